```python
import math
import jax, jax.numpy as jnp
from jax import lax
import numpy as np

D_MODEL = 4096
BATCH = 16
SEQ = 256
DEPTH = 1
DEC_BATCH = 8
DEC_SEQ = 1024
PAST_LEN = 512

GRID_W = 64
HEAD_DIM = 128
N_HEADS_A = 16
N_KV_A = 4
G_A = N_HEADS_A // N_KV_A
N_HEADS_B = 16
N_KV_B = 4
G_B = N_HEADS_B // N_KV_B
DIFF_DIM = HEAD_DIM // 2
WINDOW = 128
BLOCK = 128
D_FF = 4 * D_MODEL
ROPE_BASE = 10000.0
EPS = 1e-6
NEG = -1e30
WIDTH_A = N_HEADS_A * HEAD_DIM
WIDTH_B = N_HEADS_B * HEAD_DIM
IN_SPLITS = [WIDTH_A, N_KV_A * HEAD_DIM, N_KV_A * HEAD_DIM,
             WIDTH_B, N_KV_B * HEAD_DIM, N_KV_B * HEAD_DIM]
D_IN = sum(IN_SPLITS)

kernel_name = "hybrid_swa_sink_diffattn_dit_step"


def rms_norm(x, g):
    xf = x.astype(jnp.float32)
    y = xf * lax.rsqrt(jnp.mean(xf * xf, axis=-1, keepdims=True) + EPS)
    return (y * g.astype(jnp.float32)).astype(x.dtype)


def modulation(cvec, p, l):
    m = jax.nn.silu(cvec) @ p["w_mod"][l] + p["b_mod"][l]
    return jnp.split(m, 6, axis=-1)


def axial_angles(T, dim):
    rows = T // GRID_W
    row = jnp.repeat(jnp.arange(rows), GRID_W).astype(jnp.float32)
    col = jnp.tile(jnp.arange(GRID_W), rows).astype(jnp.float32)
    n = dim // 4
    freq = ROPE_BASE ** (-jnp.arange(n, dtype=jnp.float32) / n)
    return jnp.concatenate([row[:, None] * freq, col[:, None] * freq], axis=-1)


def rope_axial(x, ang):
    dim = x.shape[-1]
    n = dim // 4
    T = x.shape[1]
    xr = x.astype(jnp.float32).reshape(*x.shape[:-1], 2, 2, n)
    x1, x2 = xr[..., 0, :], xr[..., 1, :]
    a = ang.reshape((1, T) + (1,) * (x.ndim - 3) + (2, n))
    cos, sin = jnp.cos(a), jnp.sin(a)
    out = jnp.stack([x1 * cos - x2 * sin, x2 * cos + x1 * sin], axis=-2)
    return out.reshape(x.shape).astype(x.dtype)


def to_blocks(x):
    B, T = x.shape[:2]
    return jnp.moveaxis(x.reshape(B, T // BLOCK, BLOCK, *x.shape[2:]), 1, 0)


def from_blocks(y):
    nb, B, Q = y.shape[:3]
    return jnp.moveaxis(y, 0, 1).reshape(B, nb * Q, -1)


def project(h, p, l):
    B, T = h.shape[:2]
    P = h @ p["w_in"][l]
    q_a, k_a, v_a, q_b, k_b, v_b = jnp.split(P, np.cumsum(IN_SPLITS)[:-1].tolist(), axis=-1)
    q_a = rms_norm(q_a.reshape(B, T, N_KV_A, G_A, HEAD_DIM), p["qnorm_a_g"][l])
    k_a = rms_norm(k_a.reshape(B, T, N_KV_A, HEAD_DIM), p["knorm_a_g"][l])
    v_a = v_a.reshape(B, T, N_KV_A, HEAD_DIM)
    q_b = rms_norm(q_b.reshape(B, T, N_KV_B, G_B, 2, DIFF_DIM), p["qnorm_b_g"][l])
    k_b = rms_norm(k_b.reshape(B, T, N_KV_B, 2, DIFF_DIM), p["knorm_b_g"][l])
    v_b = v_b.reshape(B, T, N_KV_B, HEAD_DIM)
    return q_a, k_a, v_a, q_b, k_b, v_b


def gqa_sink_block(qb, k, v, mask, sink):
    s = jnp.einsum("bqkgd,bskd->bkgqs", qb, k,
                   preferred_element_type=jnp.float32) * (HEAD_DIM ** -0.5)
    if mask is not None:
        s = jnp.where(mask, s, NEG)
    sink_col = jnp.broadcast_to(sink.astype(jnp.float32)[None, :, :, None, None],
                                s.shape[:-1] + (1,))
    pr = jax.nn.softmax(jnp.concatenate([s, sink_col], axis=-1), axis=-1)[..., :-1]
    return jnp.einsum("bkgqs,bskd->bqkgd", pr.astype(v.dtype), v)


def diff_block(qb, k, v, lam, g, out_scale):
    s = jnp.einsum("bqkgcd,bskcd->bkgcqs", qb, k,
                   preferred_element_type=jnp.float32) * (DIFF_DIM ** -0.5)
    pr = jax.nn.softmax(s, axis=-1)
    a = pr[:, :, :, 0] - lam * pr[:, :, :, 1]
    o = jnp.einsum("bkgqs,bskd->bqkgd", a.astype(v.dtype), v)
    return rms_norm(o, g) * out_scale


def diff_lambda(p, l):
    lam_init = 0.8 - 0.6 * math.exp(-0.3 * l)
    lq1 = p["lam_q1"][l].astype(jnp.float32); lk1 = p["lam_k1"][l].astype(jnp.float32)
    lq2 = p["lam_q2"][l].astype(jnp.float32); lk2 = p["lam_k2"][l].astype(jnp.float32)
    lam = jnp.exp(jnp.sum(lq1 * lk1)) - jnp.exp(jnp.sum(lq2 * lk2)) + lam_init
    return lam, 1.0 - lam_init


def layer(x, cvec, p, l, ctx_kv):
    B, T = x.shape[:2]
    sh1, sc1, gt1, sh2, sc2, gt2 = modulation(cvec, p, l)
    h = rms_norm(x, p["norm1_g"][l]) * (1.0 + sc1) + sh1
    q_a, k_a, v_a, q_b, k_b, v_b = project(h, p, l)
    sink = p["sink_a"][l].reshape(N_KV_A, G_A)
    lam, out_scale = diff_lambda(p, l)
    sub_g = p["subln_g"][l]
    if ctx_kv is None:
        new_kv = (k_a, v_a, k_b, v_b)
        o_a = lax.map(lambda qb: gqa_sink_block(qb, k_a, v_a, None, sink), to_blocks(q_a))
        o_b = lax.map(lambda qb: diff_block(qb, k_b, v_b, lam, sub_g, out_scale), to_blocks(q_b))
    else:
        new_kv = None
        ck_a, cv_a, ck_b, cv_b = ctx_kv
        L = ck_a.shape[1]
        ang_a = axial_angles(T, HEAD_DIM)
        ang_b = axial_angles(T, DIFF_DIM)
        q_a = rope_axial(q_a, ang_a); k_a = rope_axial(k_a, ang_a)
        q_b = rope_axial(q_b, ang_b); k_b = rope_axial(k_b, ang_b)
        nb = T // BLOCK

        def windows(t):
            tp = jnp.pad(t, [(0, 0), (BLOCK, BLOCK)] + [(0, 0)] * (t.ndim - 2))
            tp = tp.reshape(B, nb + 2, BLOCK, *t.shape[2:])
            w = jnp.concatenate([tp[:, :-2], tp[:, 1:-1], tp[:, 2:]], axis=2)
            return jnp.moveaxis(w, 1, 0)

        qpos = jnp.arange(nb)[:, None, None] * BLOCK + jnp.arange(BLOCK)[None, :, None]
        kpos = (jnp.arange(nb)[:, None, None] - 1) * BLOCK + jnp.arange(3 * BLOCK)[None, None, :]
        mask = (jnp.abs(kpos - qpos) <= WINDOW) & (kpos >= 0) & (kpos < T)
        ctx_mask = jnp.ones((BLOCK, L), dtype=bool)

        def a_fn(args):
            qb, kw, vw, mw = args
            kk = jnp.concatenate([kw, ck_a], axis=1)
            vv = jnp.concatenate([vw, cv_a], axis=1)
            mm = jnp.concatenate([mw, ctx_mask], axis=1)
            return gqa_sink_block(qb, kk, vv, mm, sink)

        o_a = lax.map(a_fn, (to_blocks(q_a), windows(k_a), windows(v_a), mask))
        k_all = jnp.concatenate([k_b, ck_b], axis=1)
        v_all = jnp.concatenate([v_b, cv_b], axis=1)
        o_b = lax.map(lambda qb: diff_block(qb, k_all, v_all, lam, sub_g, out_scale), to_blocks(q_b))
    o = jnp.concatenate([from_blocks(o_a), from_blocks(o_b)], axis=-1) @ p["w_out"][l]
    x = x + gt1 * o
    h2 = rms_norm(x, p["norm2_g"][l]) * (1.0 + sc2) + sh2
    f = jnp.square(jax.nn.relu(h2 @ p["w_fc1"][l])) @ p["w_fc2"][l]
    x = x + gt2 * f
    return x, new_kv


def setup_inputs(seed: int = 0) -> dict:
    key = jax.random.key(seed)
    ks = jax.random.split(key, 32)
    f32 = jnp.float32

    def nrm(k, shape, scale=1.0):
        return jax.random.normal(k, shape, f32) * scale

    return {
        "x_prompt": nrm(ks[0], (BATCH, SEQ, D_MODEL)),
        "x_sample": nrm(ks[1], (DEC_BATCH, DEC_SEQ, D_MODEL)),
        "cache_a_k": nrm(ks[2], (DEC_BATCH, DEPTH, PAST_LEN, N_KV_A, HEAD_DIM)),
        "cache_a_v": nrm(ks[3], (DEC_BATCH, DEPTH, PAST_LEN, N_KV_A, HEAD_DIM)),
        "cache_b_k": nrm(ks[4], (DEC_BATCH, DEPTH, PAST_LEN, N_KV_B, 2, DIFF_DIM)),
        "cache_b_v": nrm(ks[5], (DEC_BATCH, DEPTH, PAST_LEN, N_KV_B, HEAD_DIM)),
        "c": nrm(ks[6], (DEC_BATCH, D_MODEL)),
        "c_ctx": nrm(ks[7], (D_MODEL,)),
        "w_mod": nrm(ks[8], (DEPTH, D_MODEL, 6 * D_MODEL), D_MODEL ** -0.5),
        "b_mod": nrm(ks[9], (DEPTH, 6 * D_MODEL), 0.02),
        "norm1_g": 1.0 + nrm(ks[10], (DEPTH, D_MODEL), 0.02),
        "w_in": nrm(ks[11], (DEPTH, D_MODEL, D_IN), D_MODEL ** -0.5),
        "qnorm_a_g": 1.0 + nrm(ks[12], (DEPTH, HEAD_DIM), 0.02),
        "knorm_a_g": 1.0 + nrm(ks[13], (DEPTH, HEAD_DIM), 0.02),
        "qnorm_b_g": 1.0 + nrm(ks[14], (DEPTH, DIFF_DIM), 0.02),
        "knorm_b_g": 1.0 + nrm(ks[15], (DEPTH, DIFF_DIM), 0.02),
        "sink_a": nrm(ks[16], (DEPTH, N_HEADS_A), 0.5),
        "lam_q1": nrm(ks[17], (DEPTH, DIFF_DIM), 0.1),
        "lam_k1": nrm(ks[18], (DEPTH, DIFF_DIM), 0.1),
        "lam_q2": nrm(ks[19], (DEPTH, DIFF_DIM), 0.1),
        "lam_k2": nrm(ks[20], (DEPTH, DIFF_DIM), 0.1),
        "subln_g": 1.0 + nrm(ks[21], (DEPTH, HEAD_DIM), 0.02),
        "w_out": nrm(ks[22], (DEPTH, D_MODEL, D_MODEL), D_MODEL ** -0.5),
        "norm2_g": 1.0 + nrm(ks[23], (DEPTH, D_MODEL), 0.02),
        "w_fc1": nrm(ks[24], (DEPTH, D_MODEL, D_FF), D_MODEL ** -0.5),
        "w_fc2": nrm(ks[25], (DEPTH, D_FF, D_MODEL), D_FF ** -0.5),
    }


def reference(x_prompt, x_sample, cache_a_k, cache_a_v, cache_b_k, cache_b_v, c, c_ctx,
              w_mod, b_mod, norm1_g, w_in, qnorm_a_g, knorm_a_g, qnorm_b_g, knorm_b_g,
              sink_a, lam_q1, lam_k1, lam_q2, lam_k2, subln_g, w_out, norm2_g, w_fc1, w_fc2):
    p = dict(w_mod=w_mod, b_mod=b_mod, norm1_g=norm1_g, w_in=w_in,
             qnorm_a_g=qnorm_a_g, knorm_a_g=knorm_a_g, qnorm_b_g=qnorm_b_g, knorm_b_g=knorm_b_g,
             sink_a=sink_a, lam_q1=lam_q1, lam_k1=lam_k1, lam_q2=lam_q2, lam_k2=lam_k2,
             subln_g=subln_g, w_out=w_out, norm2_g=norm2_g, w_fc1=w_fc1, w_fc2=w_fc2)
    xp = x_prompt
    cvec_ctx = c_ctx[None, None, :]
    ka_l, va_l, kb_l, vb_l = [], [], [], []
    for l in range(DEPTH):
        xp, (ka, va, kb, vb) = layer(xp, cvec_ctx, p, l, None)
        ka_l.append(ka); va_l.append(va); kb_l.append(kb); vb_l.append(vb)
    new_a_k = jnp.stack(ka_l, axis=1)
    new_a_v = jnp.stack(va_l, axis=1)
    new_b_k = jnp.stack(kb_l, axis=1)
    new_b_v = jnp.stack(vb_l, axis=1)
    xs = x_sample
    cvec = c[:, None, :]
    for l in range(DEPTH):
        ctx_kv = (cache_a_k[:, l], cache_a_v[:, l], cache_b_k[:, l], cache_b_v[:, l])
        xs, _ = layer(xs, cvec, p, l, ctx_kv)
    return (xp, xs, new_a_k, new_a_v, new_b_k, new_b_v)
```

```python
import functools
import math

import jax
import jax.numpy as jnp
from jax import lax
from jax.experimental import pallas as pl
from jax.experimental.pallas import tpu as pltpu

F32 = jnp.float32
BF16 = jnp.bfloat16

HEAD_DIM = 128
DIFF_DIM = HEAD_DIM // 2
GROUP = 4
GRID_W = 64
WINDOW = 128
BLOCK = 128
ROPE_BASE = 10000.0
EPS = 1e-6
NEG = -1e30
V7X_VMEM_LIMIT_BYTES = 56 * 1024 * 1024


def _params(*sem):
    return pltpu.CompilerParams(dimension_semantics=sem, vmem_limit_bytes=V7X_VMEM_LIMIT_BYTES)


def _tile(dim, want):
    t = min(dim, want)
    assert dim % t == 0, (dim, want)
    return t


def _mod_kernel(c_ref, w_ref, b_ref, o_ref):
    c = c_ref[...]
    a = (c * jax.nn.sigmoid(c)).astype(BF16)
    acc = jnp.dot(a, w_ref[...].astype(BF16), preferred_element_type=F32)
    o_ref[...] = acc + b_ref[...]


def _modulation(cvecs, w_mod, b_mod):
    R, D = cvecs.shape
    N = w_mod.shape[1]
    tn = _tile(N, 512)
    return pl.pallas_call(
        _mod_kernel,
        grid=(N // tn,),
        in_specs=[pl.BlockSpec((R, D), lambda n: (0, 0)),
                  pl.BlockSpec((D, tn), lambda n: (0, n)),
                  pl.BlockSpec((1, tn), lambda n: (0, n))],
        out_specs=pl.BlockSpec((R, tn), lambda n: (0, n)),
        out_shape=jax.ShapeDtypeStruct((R, N), F32),
        compiler_params=_params("arbitrary"),
        name="modulation",
    )(cvecs, w_mod, b_mod.reshape(1, N))


def _mod_spec(mod, which, tn, tm, rows_per_mod, n_axis):
    per_batch = mod.shape[0] > 1
    D = mod.shape[2] // 6
    off = which * (D // tn)

    def index(m, *rest):
        n = rest[n_axis] if n_axis is not None else 0
        return ((m * tm) // rows_per_mod if per_batch else 0, 0, off + n)

    return pl.BlockSpec((None, 1, tn), index)


def _prenorm_kernel(x_ref, g_ref, sh_ref, sc_ref, o_ref):
    x = x_ref[...]
    ms = jnp.mean(x * x, axis=-1, keepdims=True)
    y = x * lax.rsqrt(ms + EPS) * g_ref[...]
    o_ref[...] = (y * (1.0 + sc_ref[...]) + sh_ref[...]).astype(o_ref.dtype)


def _prenorm(x, g, mod, sh_idx, sc_idx, rows_per_mod):
    M, D = x.shape
    tm = _tile(M, 256)
    assert rows_per_mod % tm == 0
    return pl.pallas_call(
        _prenorm_kernel,
        grid=(M // tm,),
        in_specs=[pl.BlockSpec((tm, D), lambda m: (m, 0)),
                  pl.BlockSpec((1, D), lambda m: (0, 0)),
                  _mod_spec(mod, sh_idx, D, tm, rows_per_mod, None),
                  _mod_spec(mod, sc_idx, D, tm, rows_per_mod, None)],
        out_specs=pl.BlockSpec((tm, D), lambda m: (m, 0)),
        out_shape=jax.ShapeDtypeStruct((M, D), BF16),
        compiler_params=_params("arbitrary"),
        name="prenorm",
    )(x, g.reshape(1, D), mod, mod)


def _lane():
    return lax.broadcasted_iota(jnp.int32, (1, HEAD_DIM), 1)


def _head_norm(y, gain, chunk):
    y2 = y * y
    if chunk == HEAD_DIM:
        ms = jnp.mean(y2, axis=-1, keepdims=True)
    else:
        lo = _lane() < chunk
        s0 = jnp.sum(jnp.where(lo, y2, 0.0), axis=-1, keepdims=True)
        s1 = jnp.sum(jnp.where(lo, 0.0, y2), axis=-1, keepdims=True)
        ms = jnp.where(lo, s0, s1) * (1.0 / chunk)
    return y * lax.rsqrt(ms + EPS) * gain


def _rope(y, cos, sin, chunk):
    rot = chunk // 4
    first = (_lane() % (2 * rot)) < rot
    partner = jnp.where(first, pltpu.roll(y, HEAD_DIM - rot, 1), pltpu.roll(y, rot, 1))
    return y * cos + partner * sin


def _qproj_kernel(*refs, chunk, scale, rope):
    if rope:
        a_ref, w_ref, g_ref, cos_ref, sin_ref, o_ref = refs
    else:
        a_ref, w_ref, g_ref, o_ref = refs
    acc = jnp.dot(a_ref[...], w_ref[...], preferred_element_type=F32)
    for h in range(GROUP):
        y = _head_norm(acc[:, h * HEAD_DIM:(h + 1) * HEAD_DIM], g_ref[...], chunk)
        if rope:
            y = _rope(y, cos_ref[...], sin_ref[...], chunk)
        o_ref[h] = (y * scale).astype(o_ref.dtype)


def _q_proj(h, w_in, col0, n_kv, gain, chunk, tables, seq):
    M, D = h.shape
    tm = _tile(seq, 1024) if tables is not None else _tile(M, 1024)
    tn = GROUP * HEAD_DIM
    assert col0 % tn == 0
    rope = tables is not None
    in_specs = [pl.BlockSpec((tm, D), lambda m, n: (m, 0)),
                pl.BlockSpec((D, tn), lambda m, n: (0, col0 // tn + n)),
                pl.BlockSpec((1, HEAD_DIM), lambda m, n: (0, 0))]
    args = [h, w_in, gain]
    if rope:
        per_seq = seq // tm
        in_specs += [pl.BlockSpec((tm, HEAD_DIM), lambda m, n: (m % per_seq, 0))] * 2
        args += list(tables)
    return pl.pallas_call(
        functools.partial(_qproj_kernel, chunk=chunk, scale=chunk ** -0.5, rope=rope),
        grid=(M // tm, n_kv),
        in_specs=in_specs,
        out_specs=pl.BlockSpec((None, GROUP, tm, HEAD_DIM), lambda m, n: (n, 0, m, 0)),
        out_shape=jax.ShapeDtypeStruct((n_kv, GROUP, M, HEAD_DIM), BF16),
        compiler_params=_params("arbitrary", "arbitrary"),
        name="q_proj",
    )(*args)


def _kvproj_kernel(*refs, chunk, rope, n_kv):
    if rope:
        a_ref, w_ref, g_ref, cos_ref, sin_ref, k_ref, v_ref = refs
    else:
        a_ref, w_ref, g_ref, k_ref, v_ref = refs
    acc = jnp.dot(a_ref[...], w_ref[...], preferred_element_type=F32)
    width = n_kv * HEAD_DIM
    for h in range(n_kv):
        sl = slice(h * HEAD_DIM, (h + 1) * HEAD_DIM)
        y = _head_norm(acc[:, sl], g_ref[...], chunk)
        if rope:
            y = _rope(y, cos_ref[...], sin_ref[...], chunk)
        k_ref[:, sl] = y.astype(k_ref.dtype)
    v_ref[...] = acc[:, width:].astype(v_ref.dtype)


def _kv_proj(h, w_in, col0, n_kv, gain, chunk, tables, seq, out_dtype):
    M, D = h.shape
    tm = _tile(seq, 1024) if tables is not None else _tile(M, 1024)
    width = n_kv * HEAD_DIM
    tn = 2 * width
    assert col0 % tn == 0
    rope = tables is not None
    in_specs = [pl.BlockSpec((tm, D), lambda m: (m, 0)),
                pl.BlockSpec((D, tn), lambda m: (0, col0 // tn)),
                pl.BlockSpec((1, HEAD_DIM), lambda m: (0, 0))]
    args = [h, w_in, gain]
    if rope:
        per_seq = seq // tm
        in_specs += [pl.BlockSpec((tm, HEAD_DIM), lambda m: (m % per_seq, 0))] * 2
        args += list(tables)
    return pl.pallas_call(
        functools.partial(_kvproj_kernel, chunk=chunk, rope=rope, n_kv=n_kv),
        grid=(M // tm,),
        in_specs=in_specs,
        out_specs=[pl.BlockSpec((tm, width), lambda m: (m, 0))] * 2,
        out_shape=[jax.ShapeDtypeStruct((M, width), out_dtype)] * 2,
        compiler_params=_params("arbitrary"),
        name="kv_proj",
    )(*args)


def _softmax_sink_pv(scores, values, sink):
    m = sink
    for s in scores:
        m = jnp.maximum(m, jnp.max(s, axis=-1, keepdims=True))
    denom = jnp.exp(sink - m)
    o = None
    for s, v in zip(scores, values):
        p = jnp.exp(s - m)
        denom = denom + jnp.sum(p, axis=-1, keepdims=True)
        pv = jnp.dot(p.astype(BF16), v, preferred_element_type=F32)
        o = pv if o is None else o + pv
    return o / denom


def _qk(q, k):
    return lax.dot_general(q, k, (((1,), (1,)), ((), ())), preferred_element_type=F32)


def _attn_a_ctx_kernel(sink_ref, q_ref, k_ref, v_ref, o_ref):
    kv = pl.program_id(1)
    k = k_ref[...].astype(BF16)
    v = v_ref[...].astype(BF16)
    for g in range(GROUP):
        sink = jnp.full((1, 1), sink_ref[kv * GROUP + g], F32)
        o = _softmax_sink_pv([_qk(q_ref[g], k)], [v], sink)
        o_ref[:, g * HEAD_DIM:(g + 1) * HEAD_DIM] = o.astype(o_ref.dtype)


def _attn_a_ctx(q, k, v, sink, batch, seq):
    n_kv, _, M, _ = q.shape
    return pl.pallas_call(
        _attn_a_ctx_kernel,
        grid=(batch, n_kv),
        in_specs=[pl.BlockSpec(memory_space=pltpu.SMEM),
                  pl.BlockSpec((None, GROUP, seq, HEAD_DIM), lambda b, h: (h, 0, b, 0)),
                  pl.BlockSpec((seq, HEAD_DIM), lambda b, h: (b, h)),
                  pl.BlockSpec((seq, HEAD_DIM), lambda b, h: (b, h))],
        out_specs=pl.BlockSpec((seq, GROUP * HEAD_DIM), lambda b, h: (b, h)),
        out_shape=jax.ShapeDtypeStruct((M, n_kv * GROUP * HEAD_DIM), BF16),
        compiler_params=_params("arbitrary", "arbitrary"),
        name="attn_a_ctx",
    )(sink, q, k, v)


def _attn_a_lat_kernel(sink_ref, q_ref, k_ref, v_ref, ck_ref, cv_ref, o_ref, *, n_blocks):
    kv = pl.program_id(1)
    j = pl.program_id(2)
    rows = GROUP * BLOCK
    span = 3 * BLOCK
    start = pl.multiple_of(jnp.clip(j - 1, 0, n_blocks - 3) * BLOCK, BLOCK)
    kw = k_ref[pl.ds(start, span), :]
    vw = v_ref[pl.ds(start, span), :]
    kc = ck_ref[...].astype(BF16)
    vc = cv_ref[...].astype(BF16)
    q = q_ref[...].reshape(rows, HEAD_DIM)

    row = lax.broadcasted_iota(jnp.int32, (rows, 1), 0)
    qpos = j * BLOCK + row % BLOCK
    kpos = start + lax.broadcasted_iota(jnp.int32, (1, span), 1)
    s_w = jnp.where(jnp.abs(kpos - qpos) <= WINDOW, _qk(q, kw), NEG)
    s_c = _qk(q, kc)

    sink = jnp.zeros((rows, 1), F32)
    for g in range(GROUP):
        sink = jnp.where(row // BLOCK == g, sink_ref[kv * GROUP + g], sink)
    o = _softmax_sink_pv([s_w, s_c], [vw, vc], sink)
    for g in range(GROUP):
        o_ref[:, g * HEAD_DIM:(g + 1) * HEAD_DIM] = o[g * BLOCK:(g + 1) * BLOCK].astype(o_ref.dtype)


def _attn_a_lat(q, k, v, ck, cv, sink, batch, seq):
    n_kv, _, M, _ = q.shape
    n_blocks = seq // BLOCK
    assert n_blocks >= 3
    L = ck.shape[1]
    return pl.pallas_call(
        functools.partial(_attn_a_lat_kernel, n_blocks=n_blocks),
        grid=(batch, n_kv, n_blocks),
        in_specs=[pl.BlockSpec(memory_space=pltpu.SMEM),
                  pl.BlockSpec((None, GROUP, BLOCK, HEAD_DIM), lambda b, h, j: (h, 0, b * n_blocks + j, 0)),
                  pl.BlockSpec((seq, HEAD_DIM), lambda b, h, j: (b, h)),
                  pl.BlockSpec((seq, HEAD_DIM), lambda b, h, j: (b, h)),
                  pl.BlockSpec((None, L, HEAD_DIM), lambda b, h, j: (b, 0, h)),
                  pl.BlockSpec((None, L, HEAD_DIM), lambda b, h, j: (b, 0, h))],
        out_specs=pl.BlockSpec((BLOCK, GROUP * HEAD_DIM), lambda b, h, j: (b * n_blocks + j, h)),
        out_shape=jax.ShapeDtypeStruct((M, n_kv * GROUP * HEAD_DIM), BF16),
        compiler_params=_params("arbitrary", "arbitrary", "arbitrary"),
        name="attn_a_lat",
    )(sink, q, k, v, ck, cv)


def _attn_b_kernel(*refs, has_ctx, lam_init):
    if has_ctx:
        lam_ref, subg_ref, q_ref, k_ref, v_ref, ck_ref, cv_ref, o_ref = refs
    else:
        lam_ref, subg_ref, q_ref, k_ref, v_ref, o_ref = refs
    lv = lam_ref[...]
    lam = (jnp.exp(jnp.sum(lv[0:1] * lv[1:2], axis=-1, keepdims=True))
           - jnp.exp(jnp.sum(lv[2:3] * lv[3:4], axis=-1, keepdims=True)) + lam_init)

    k = k_ref[...].astype(F32)
    v = v_ref[...].astype(BF16)
    if has_ctx:
        k = jnp.concatenate([k, ck_ref[...]], axis=0)
        v = jnp.concatenate([v, cv_ref[...].astype(BF16)], axis=0)
    lo = _lane() < DIFF_DIM
    k0 = jnp.where(lo, k, 0.0).astype(BF16)
    k1 = jnp.where(lo, 0.0, k).astype(BF16)

    for g in range(GROUP):
        q = q_ref[g]
        parts = []
        for kc in (k0, k1):
            s = _qk(q, kc)
            e = jnp.exp(s - jnp.max(s, axis=-1, keepdims=True))
            parts.append((e, 1.0 / jnp.sum(e, axis=-1, keepdims=True)))
        (e0, r0), (e1, r1) = parts
        a = e0 * r0 - e1 * (lam * r1)
        o = jnp.dot(a.astype(BF16), v, preferred_element_type=F32)
        ms = jnp.mean(o * o, axis=-1, keepdims=True)
        o = o * lax.rsqrt(ms + EPS) * subg_ref[...] * (1.0 - lam_init)
        o_ref[:, g * HEAD_DIM:(g + 1) * HEAD_DIM] = o.astype(o_ref.dtype)


def _attn_b(q, k, v, ctx, lam_vecs, sub_g, lam_init, batch, seq, tq):
    n_kv, _, M, _ = q.shape
    n_q = seq // tq
    has_ctx = ctx is not None
    in_specs = [pl.BlockSpec((4, DIFF_DIM), lambda b, h, j: (0, 0)),
                pl.BlockSpec((1, HEAD_DIM), lambda b, h, j: (0, 0)),
                pl.BlockSpec((None, GROUP, tq, HEAD_DIM), lambda b, h, j: (h, 0, b * n_q + j, 0)),
                pl.BlockSpec((seq, HEAD_DIM), lambda b, h, j: (b, h)),
                pl.BlockSpec((seq, HEAD_DIM), lambda b, h, j: (b, h))]
    args = [lam_vecs, sub_g, q, k, v]
    if has_ctx:
        L = ctx[0].shape[1]
        in_specs += [pl.BlockSpec((None, L, HEAD_DIM), lambda b, h, j: (b, 0, h))] * 2
        args += list(ctx)
    return pl.pallas_call(
        functools.partial(_attn_b_kernel, has_ctx=has_ctx, lam_init=lam_init),
        grid=(batch, n_kv, n_q),
        in_specs=in_specs,
        out_specs=pl.BlockSpec((tq, GROUP * HEAD_DIM), lambda b, h, j: (b * n_q + j, h)),
        out_shape=jax.ShapeDtypeStruct((M, n_kv * GROUP * HEAD_DIM), BF16),
        compiler_params=_params("arbitrary", "arbitrary", "arbitrary"),
        name="attn_b",
    )(*args)


def _oproj_kernel(oa_ref, ob_ref, wa_ref, wb_ref, x_ref, gt_ref, y_ref):
    acc = jnp.dot(oa_ref[...], wa_ref[...], preferred_element_type=F32)
    acc = acc + jnp.dot(ob_ref[...], wb_ref[...], preferred_element_type=F32)
    y_ref[...] = x_ref[...] + gt_ref[...] * acc


def _out_proj(o_a, o_b, w_out, x, mod, gt_idx, rows_per_mod):
    M, D = x.shape
    wa = o_a.shape[1]
    wb = o_b.shape[1]
    assert wa == wb and wa + wb == w_out.shape[0]
    tm = _tile(rows_per_mod, 1024) if mod.shape[0] > 1 else _tile(M, 1024)
    tn = _tile(D, 512)
    return pl.pallas_call(
        _oproj_kernel,
        grid=(M // tm, D // tn),
        in_specs=[pl.BlockSpec((tm, wa), lambda m, n: (m, 0)),
                  pl.BlockSpec((tm, wb), lambda m, n: (m, 0)),
                  pl.BlockSpec((wa, tn), lambda m, n: (0, n)),
                  pl.BlockSpec((wb, tn), lambda m, n: (1, n)),
                  pl.BlockSpec((tm, tn), lambda m, n: (m, n)),
                  _mod_spec(mod, gt_idx, tn, tm, rows_per_mod, 0)],
        out_specs=pl.BlockSpec((tm, tn), lambda m, n: (m, n)),
        out_shape=jax.ShapeDtypeStruct((M, D), F32),
        compiler_params=_params("arbitrary", "arbitrary"),
        name="out_proj",
    )(o_a, o_b, w_out, w_out, x, mod)


def _fc1_kernel(a_ref, w_ref, o_ref):
    acc = jnp.dot(a_ref[...], w_ref[...], preferred_element_type=F32)
    r = jnp.maximum(acc, 0.0)
    o_ref[...] = (r * r).astype(o_ref.dtype)


def _fc1(h, w):
    M, D = h.shape
    N = w.shape[1]
    tm = _tile(M, 1024)
    tn = _tile(N, 1024)
    return pl.pallas_call(
        _fc1_kernel,
        grid=(M // tm, N // tn),
        in_specs=[pl.BlockSpec((tm, D), lambda m, n: (m, 0)),
                  pl.BlockSpec((D, tn), lambda m, n: (0, n))],
        out_specs=pl.BlockSpec((tm, tn), lambda m, n: (m, n)),
        out_shape=jax.ShapeDtypeStruct((M, N), BF16),
        compiler_params=_params("arbitrary", "arbitrary"),
        name="fc1",
    )(h, w)


def _fc2_kernel(a_ref, w_ref, x_ref, gt_ref, y_ref, acc_ref):
    k = pl.program_id(2)
    part = jnp.dot(a_ref[...], w_ref[...], preferred_element_type=F32)

    @pl.when(k == 0)
    def _():
        acc_ref[...] = part

    @pl.when(k > 0)
    def _():
        acc_ref[...] += part

    @pl.when(k == pl.num_programs(2) - 1)
    def _():
        y_ref[...] = x_ref[...] + gt_ref[...] * acc_ref[...]


def _fc2(hid, w, x, mod, gt_idx, rows_per_mod):
    M, D = x.shape
    K = hid.shape[1]
    tm = _tile(rows_per_mod, 1024) if mod.shape[0] > 1 else _tile(M, 1024)
    tn = _tile(D, 1024)
    tk = _tile(K, 2048)
    return pl.pallas_call(
        _fc2_kernel,
        grid=(M // tm, D // tn, K // tk),
        in_specs=[pl.BlockSpec((tm, tk), lambda m, n, k: (m, k)),
                  pl.BlockSpec((tk, tn), lambda m, n, k: (k, n)),
                  pl.BlockSpec((tm, tn), lambda m, n, k: (m, n)),
                  _mod_spec(mod, gt_idx, tn, tm, rows_per_mod, 0)],
        out_specs=pl.BlockSpec((tm, tn), lambda m, n, k: (m, n)),
        out_shape=jax.ShapeDtypeStruct((M, D), F32),
        scratch_shapes=[pltpu.VMEM((tm, tn), F32)],
        compiler_params=_params("arbitrary", "arbitrary", "arbitrary"),
        name="fc2",
    )(hid, w, x, mod)


def _rope_tables(seq, chunk):
    n = chunk // 4
    pos = jnp.arange(seq)
    row = (pos // GRID_W).astype(F32)
    col = (pos % GRID_W).astype(F32)
    freq = ROPE_BASE ** (-jnp.arange(n, dtype=F32) / n)
    ra = row[:, None] * freq
    ca = col[:, None] * freq
    cos = jnp.concatenate([jnp.cos(ra)] * 2 + [jnp.cos(ca)] * 2, axis=-1)
    sin = jnp.concatenate([-jnp.sin(ra), jnp.sin(ra), -jnp.sin(ca), jnp.sin(ca)], axis=-1)
    reps = HEAD_DIM // chunk
    return jnp.tile(cos, (1, reps)), jnp.tile(sin, (1, reps))


def _layer(x, mod, w, l, batch, seq, n_kv_a, n_kv_b, ctx_kv):
    latent = ctx_kv is not None
    wa = n_kv_a * GROUP * HEAD_DIM
    ka = n_kv_a * HEAD_DIM
    wb = n_kv_b * GROUP * HEAD_DIM
    tab_a = _rope_tables(seq, HEAD_DIM) if latent else None
    tab_b = _rope_tables(seq, DIFF_DIM) if latent else None
    kv_dtype = BF16 if latent else F32

    h = _prenorm(x, w["norm1_g"], mod, 0, 1, seq)
    gq_a = w["qnorm_a_g"].reshape(1, HEAD_DIM)
    gk_a = w["knorm_a_g"].reshape(1, HEAD_DIM)
    gq_b = jnp.tile(w["qnorm_b_g"], 2).reshape(1, HEAD_DIM)
    gk_b = jnp.tile(w["knorm_b_g"], 2).reshape(1, HEAD_DIM)
    q_a = _q_proj(h, w["w_in"], 0, n_kv_a, gq_a, HEAD_DIM, tab_a, seq)
    k_a, v_a = _kv_proj(h, w["w_in"], wa, n_kv_a, gk_a, HEAD_DIM, tab_a, seq, kv_dtype)
    q_b = _q_proj(h, w["w_in"], wa + 2 * ka, n_kv_b, gq_b, DIFF_DIM, tab_b, seq)
    k_b, v_b = _kv_proj(h, w["w_in"], wa + 2 * ka + wb, n_kv_b, gk_b, DIFF_DIM, tab_b, seq, kv_dtype)

    lam_vecs = jnp.stack([w["lam_q1"], w["lam_k1"], w["lam_q2"], w["lam_k2"]]).astype(F32)
    lam_init = 0.8 - 0.6 * math.exp(-0.3 * l)
    sub_g = w["subln_g"].reshape(1, HEAD_DIM)
    if latent:
        ck_a, cv_a, ck_b, cv_b = ctx_kv
        o_a = _attn_a_lat(q_a, k_a, v_a, ck_a, cv_a, w["sink_a"], batch, seq)
        o_b = _attn_b(q_b, k_b, v_b, (ck_b, cv_b), lam_vecs, sub_g, lam_init, batch, seq, _tile(seq, 256))
    else:
        o_a = _attn_a_ctx(q_a, k_a, v_a, w["sink_a"], batch, seq)
        o_b = _attn_b(q_b, k_b, v_b, None, lam_vecs, sub_g, lam_init, batch, seq, _tile(seq, 256))

    x = _out_proj(o_a, o_b, w["w_out"], x, mod, 2, seq)
    h2 = _prenorm(x, w["norm2_g"], mod, 3, 4, seq)
    hid = _fc1(h2, w["w_fc1"])
    x = _fc2(hid, w["w_fc2"], x, mod, 5, seq)
    return x, (k_a, v_a, k_b, v_b)


def kernel(x_prompt, x_sample, cache_a_k, cache_a_v, cache_b_k, cache_b_v, c, c_ctx, w_mod, b_mod, norm1_g, w_in, qnorm_a_g, knorm_a_g, qnorm_b_g, knorm_b_g, sink_a, lam_q1, lam_k1, lam_q2, lam_k2, subln_g, w_out, norm2_g, w_fc1, w_fc2):
    batch, seq, D = x_prompt.shape
    dbatch, dseq, _ = x_sample.shape
    depth = w_mod.shape[0]
    past = cache_a_k.shape[2]
    n_kv_a = cache_a_k.shape[3]
    n_kv_b = cache_b_k.shape[3]
    assert dseq % GRID_W == 0 and dseq % BLOCK == 0

    per_layer = dict(norm1_g=norm1_g, qnorm_a_g=qnorm_a_g, knorm_a_g=knorm_a_g, qnorm_b_g=qnorm_b_g,
                     knorm_b_g=knorm_b_g, sink_a=sink_a, lam_q1=lam_q1, lam_k1=lam_k1, lam_q2=lam_q2,
                     lam_k2=lam_k2, subln_g=subln_g, norm2_g=norm2_g)
    big = dict(w_in=w_in, w_out=w_out, w_fc1=w_fc1, w_fc2=w_fc2)

    n_rows = -(-(dbatch + 1) // 8) * 8
    cvecs = jnp.zeros((n_rows, D), F32).at[:dbatch].set(c).at[dbatch].set(c_ctx)

    xp = x_prompt.reshape(batch * seq, D)
    xs = x_sample.reshape(dbatch * dseq, D)
    new_kv = []
    for l in range(depth):
        w = {k: v[l] for k, v in per_layer.items()}
        w.update({k: v[l].astype(BF16) for k, v in big.items()})
        mod = _modulation(cvecs, w_mod[l], b_mod[l])
        mod_ctx = mod[dbatch:dbatch + 1].reshape(1, 1, 6 * D)
        mod_lat = mod[:dbatch].reshape(dbatch, 1, 6 * D)

        xp, kv = _layer(xp, mod_ctx, w, l, batch, seq, n_kv_a, n_kv_b, None)
        new_kv.append(kv)
        ctx_kv = (cache_a_k[:, l].reshape(dbatch, past, n_kv_a * HEAD_DIM),
                  cache_a_v[:, l].reshape(dbatch, past, n_kv_a * HEAD_DIM),
                  cache_b_k[:, l].reshape(dbatch, past, n_kv_b * HEAD_DIM),
                  cache_b_v[:, l].reshape(dbatch, past, n_kv_b * HEAD_DIM))
        xs, _ = _layer(xs, mod_lat, w, l, dbatch, dseq, n_kv_a, n_kv_b, ctx_kv)

    def stack(i, shape):
        return jnp.stack([kv[i].reshape((batch, seq) + shape) for kv in new_kv], axis=1)

    return (xp.reshape(batch, seq, D), xs.reshape(dbatch, dseq, D),
            stack(0, (n_kv_a, HEAD_DIM)), stack(1, (n_kv_a, HEAD_DIM)),
            stack(2, (n_kv_b, 2, DIFF_DIM)), stack(3, (n_kv_b, HEAD_DIM)))
```

```python
import functools
import math

import jax
import jax.numpy as jnp
from jax import lax
from jax.experimental import pallas as pl
from jax.experimental.pallas import tpu as pltpu

F32 = jnp.float32
BF16 = jnp.bfloat16

HEAD_DIM = 128
DIFF_DIM = HEAD_DIM // 2
GROUP = 4
GRID_W = 64
WINDOW = 128
BLOCK = 128
ROPE_BASE = 10000.0
EPS = 1e-6
NEG = -1e30
LOG2E = math.log2(math.e)
V7X_VMEM_LIMIT_BYTES = 56 * 1024 * 1024


def _params(*sem):
    return pltpu.CompilerParams(dimension_semantics=sem, vmem_limit_bytes=V7X_VMEM_LIMIT_BYTES)


def _tile(dim, want):
    t = min(dim, want)
    assert dim % t == 0, (dim, want)
    return t


def _mod_kernel(c_ref, w_ref, b_ref, o_ref):
    c = c_ref[...]
    a = (c * jax.nn.sigmoid(c)).astype(BF16)
    acc = jnp.dot(a, w_ref[...].astype(BF16), preferred_element_type=F32)
    o_ref[...] = acc + b_ref[...]


def _modulation(cvecs, w_mod, b_mod):
    R, D = cvecs.shape
    N = w_mod.shape[1]
    tn = _tile(N, 512)
    return pl.pallas_call(
        _mod_kernel,
        grid=(N // tn,),
        in_specs=[pl.BlockSpec((R, D), lambda n: (0, 0)),
                  pl.BlockSpec((D, tn), lambda n: (0, n)),
                  pl.BlockSpec((1, tn), lambda n: (0, n))],
        out_specs=pl.BlockSpec((R, tn), lambda n: (0, n)),
        out_shape=jax.ShapeDtypeStruct((R, N), F32),
        compiler_params=_params("arbitrary"),
        name="modulation",
    )(cvecs, w_mod, b_mod.reshape(1, N))


def _mod_spec(mod, which, tn, tm, rows_per_mod, n_axis):
    per_batch = mod.shape[0] > 1
    D = mod.shape[2] // 6
    off = which * (D // tn)

    def index(m, *rest):
        n = rest[n_axis] if n_axis is not None else 0
        return ((m * tm) // rows_per_mod if per_batch else 0, 0, off + n)

    return pl.BlockSpec((None, 1, tn), index)


def _prenorm_kernel(x_ref, g_ref, sh_ref, sc_ref, o_ref):
    x = x_ref[...]
    ms = jnp.mean(x * x, axis=-1, keepdims=True)
    y = x * lax.rsqrt(ms + EPS) * g_ref[...]
    o_ref[...] = (y * (1.0 + sc_ref[...]) + sh_ref[...]).astype(o_ref.dtype)


def _prenorm(x, g, mod, sh_idx, sc_idx, rows_per_mod):
    M, D = x.shape
    tm = _tile(M, 256)
    assert rows_per_mod % tm == 0
    return pl.pallas_call(
        _prenorm_kernel,
        grid=(M // tm,),
        in_specs=[pl.BlockSpec((tm, D), lambda m: (m, 0)),
                  pl.BlockSpec((1, D), lambda m: (0, 0)),
                  _mod_spec(mod, sh_idx, D, tm, rows_per_mod, None),
                  _mod_spec(mod, sc_idx, D, tm, rows_per_mod, None)],
        out_specs=pl.BlockSpec((tm, D), lambda m: (m, 0)),
        out_shape=jax.ShapeDtypeStruct((M, D), BF16),
        compiler_params=_params("arbitrary"),
        name="prenorm",
    )(x, g.reshape(1, D), mod, mod)


def _lane():
    return lax.broadcasted_iota(jnp.int32, (1, HEAD_DIM), 1)


def _head_norm(y, gain, chunk):
    y2 = y * y
    if chunk == HEAD_DIM:
        ms = jnp.mean(y2, axis=-1, keepdims=True)
    else:
        lo = _lane() < chunk
        s0 = jnp.sum(jnp.where(lo, y2, 0.0), axis=-1, keepdims=True)
        s1 = jnp.sum(jnp.where(lo, 0.0, y2), axis=-1, keepdims=True)
        ms = jnp.where(lo, s0, s1) * (1.0 / chunk)
    return y * lax.rsqrt(ms + EPS) * gain


def _rope(y, cos, sin, chunk):
    rot = chunk // 4
    first = (_lane() % (2 * rot)) < rot
    partner = jnp.where(first, pltpu.roll(y, HEAD_DIM - rot, 1), pltpu.roll(y, rot, 1))
    return y * cos + partner * sin


PROJ_ROW_CHUNK = 256


def _row_chunks(rows):
    step = min(rows, PROJ_ROW_CHUNK)
    return [slice(r, r + step) for r in range(0, rows, step)]


def _qproj_kernel(*refs, chunk, scale, rope):
    if rope:
        a_ref, w_ref, g_ref, cos_ref, sin_ref, o_ref = refs
    else:
        a_ref, w_ref, g_ref, o_ref = refs
    for rows in _row_chunks(a_ref.shape[0]):
        acc = jnp.dot(a_ref[rows, :], w_ref[...], preferred_element_type=F32)
        for h in range(GROUP):
            y = _head_norm(acc[:, h * HEAD_DIM:(h + 1) * HEAD_DIM], g_ref[...], chunk)
            if rope:
                y = _rope(y, cos_ref[rows, :], sin_ref[rows, :], chunk)
            o_ref[h, rows, :] = (y * scale).astype(o_ref.dtype)


def _q_proj(h, w_in, col0, n_kv, gain, chunk, tables, seq):
    M, D = h.shape
    tm = _tile(seq, 1024) if tables is not None else _tile(M, 1024)
    tn = GROUP * HEAD_DIM
    assert col0 % tn == 0
    rope = tables is not None
    in_specs = [pl.BlockSpec((tm, D), lambda m, n: (m, 0)),
                pl.BlockSpec((D, tn), lambda m, n: (0, col0 // tn + n)),
                pl.BlockSpec((1, HEAD_DIM), lambda m, n: (0, 0))]
    args = [h, w_in, gain]
    if rope:
        per_seq = seq // tm
        in_specs += [pl.BlockSpec((tm, HEAD_DIM), lambda m, n: (m % per_seq, 0))] * 2
        args += list(tables)
    return pl.pallas_call(
        functools.partial(_qproj_kernel, chunk=chunk, scale=chunk ** -0.5 * LOG2E, rope=rope),
        grid=(M // tm, n_kv),
        in_specs=in_specs,
        out_specs=pl.BlockSpec((None, GROUP, tm, HEAD_DIM), lambda m, n: (n, 0, m, 0)),
        out_shape=jax.ShapeDtypeStruct((n_kv, GROUP, M, HEAD_DIM), BF16),
        compiler_params=_params("arbitrary", "arbitrary"),
        name="q_proj",
    )(*args)


def _kvproj_kernel(*refs, chunk, rope, n_kv):
    if rope:
        a_ref, w_ref, g_ref, cos_ref, sin_ref, k_ref, v_ref = refs
    else:
        a_ref, w_ref, g_ref, k_ref, v_ref = refs
    width = n_kv * HEAD_DIM
    for rows in _row_chunks(a_ref.shape[0]):
        acc = jnp.dot(a_ref[rows, :], w_ref[...], preferred_element_type=F32)
        for h in range(n_kv):
            sl = slice(h * HEAD_DIM, (h + 1) * HEAD_DIM)
            y = _head_norm(acc[:, sl], g_ref[...], chunk)
            if rope:
                y = _rope(y, cos_ref[rows, :], sin_ref[rows, :], chunk)
            k_ref[rows, sl] = y.astype(k_ref.dtype)
        v_ref[rows, :] = acc[:, width:].astype(v_ref.dtype)


def _kv_proj(h, w_in, col0, n_kv, gain, chunk, tables, seq, out_dtype):
    M, D = h.shape
    tm = _tile(seq, 1024) if tables is not None else _tile(M, 1024)
    width = n_kv * HEAD_DIM
    tn = 2 * width
    assert col0 % tn == 0
    rope = tables is not None
    in_specs = [pl.BlockSpec((tm, D), lambda m: (m, 0)),
                pl.BlockSpec((D, tn), lambda m: (0, col0 // tn)),
                pl.BlockSpec((1, HEAD_DIM), lambda m: (0, 0))]
    args = [h, w_in, gain]
    if rope:
        per_seq = seq // tm
        in_specs += [pl.BlockSpec((tm, HEAD_DIM), lambda m: (m % per_seq, 0))] * 2
        args += list(tables)
    return pl.pallas_call(
        functools.partial(_kvproj_kernel, chunk=chunk, rope=rope, n_kv=n_kv),
        grid=(M // tm,),
        in_specs=in_specs,
        out_specs=[pl.BlockSpec((tm, width), lambda m: (m, 0))] * 2,
        out_shape=[jax.ShapeDtypeStruct((M, width), out_dtype)] * 2,
        compiler_params=_params("arbitrary"),
        name="kv_proj",
    )(*args)


def _with_ones(v):
    return jnp.concatenate([v, jnp.ones_like(v)], axis=1)


def _weights(s, m):
    return jnp.exp2((s - m).astype(BF16))


def _softmax_sink_pv(scores, values_ones, sink):
    m = sink
    for s in scores:
        m = jnp.maximum(m, jnp.max(s, axis=-1, keepdims=True))
    ox = None
    for s, vx in zip(scores, values_ones):
        pv = jnp.dot(_weights(s, m), vx, preferred_element_type=F32)
        ox = pv if ox is None else ox + pv
    denom = ox[:, HEAD_DIM:] + jnp.exp2(sink - m)
    return ox[:, :HEAD_DIM] / denom


def _qk(q, k):
    return lax.dot_general(q, k, (((1,), (1,)), ((), ())), preferred_element_type=F32)


def _attn_a_ctx_kernel(sink_ref, q_ref, k_ref, v_ref, o_ref, *, n_kv):
    for h in range(n_kv):
        cols = slice(h * HEAD_DIM, (h + 1) * HEAD_DIM)
        k = k_ref[:, cols].astype(BF16)
        vx = _with_ones(v_ref[:, cols].astype(BF16))
        for g in range(GROUP):
            head = h * GROUP + g
            sink = jnp.full((1, 1), sink_ref[head] * LOG2E, F32)
            o = _softmax_sink_pv([_qk(q_ref[h, g], k)], [vx], sink)
            o_ref[:, head * HEAD_DIM:(head + 1) * HEAD_DIM] = o.astype(o_ref.dtype)


def _attn_a_ctx(q, k, v, sink, batch, seq):
    n_kv, _, M, _ = q.shape
    width = n_kv * HEAD_DIM
    return pl.pallas_call(
        functools.partial(_attn_a_ctx_kernel, n_kv=n_kv),
        grid=(batch,),
        in_specs=[pl.BlockSpec(memory_space=pltpu.SMEM),
                  pl.BlockSpec((n_kv, GROUP, seq, HEAD_DIM), lambda b: (0, 0, b, 0)),
                  pl.BlockSpec((seq, width), lambda b: (b, 0)),
                  pl.BlockSpec((seq, width), lambda b: (b, 0))],
        out_specs=pl.BlockSpec((seq, GROUP * width), lambda b: (b, 0)),
        out_shape=jax.ShapeDtypeStruct((M, GROUP * width), BF16),
        compiler_params=_params("arbitrary"),
        name="attn_a_ctx",
    )(sink, q, k, v)


def _attn_a_lat_kernel(sink_ref, q_ref, k_ref, v_ref, ck_ref, cv_ref, o_ref, *, n_blocks, n_kv):
    j = pl.program_id(1)
    rows = GROUP * BLOCK
    span = 3 * BLOCK
    start = pl.multiple_of(jnp.clip(j - 1, 0, n_blocks - 3) * BLOCK, BLOCK)
    row = lax.broadcasted_iota(jnp.int32, (rows, 1), 0)
    qpos = j * BLOCK + row % BLOCK
    kpos = start + lax.broadcasted_iota(jnp.int32, (1, span), 1)
    in_window = jnp.abs(kpos - qpos) <= WINDOW

    for h in range(n_kv):
        cols = slice(h * HEAD_DIM, (h + 1) * HEAD_DIM)
        kw = k_ref[pl.ds(start, span), cols]
        vw = _with_ones(v_ref[pl.ds(start, span), cols])
        kc = ck_ref[:, cols].astype(BF16)
        vc = _with_ones(cv_ref[:, cols].astype(BF16))
        q = q_ref[h].reshape(rows, HEAD_DIM)
        s_w = jnp.where(in_window, _qk(q, kw), NEG)
        s_c = _qk(q, kc)
        sink = jnp.zeros((rows, 1), F32)
        for g in range(GROUP):
            sink = jnp.where(row // BLOCK == g, sink_ref[h * GROUP + g] * LOG2E, sink)
        o = _softmax_sink_pv([s_w, s_c], [vw, vc], sink)
        for g in range(GROUP):
            head = h * GROUP + g
            o_ref[:, head * HEAD_DIM:(head + 1) * HEAD_DIM] = o[g * BLOCK:(g + 1) * BLOCK].astype(o_ref.dtype)


def _attn_a_lat(q, k, v, ck, cv, sink, batch, seq):
    n_kv, _, M, _ = q.shape
    n_blocks = seq // BLOCK
    assert n_blocks >= 3
    L = ck.shape[1]
    width = n_kv * HEAD_DIM
    return pl.pallas_call(
        functools.partial(_attn_a_lat_kernel, n_blocks=n_blocks, n_kv=n_kv),
        grid=(batch, n_blocks),
        in_specs=[pl.BlockSpec(memory_space=pltpu.SMEM),
                  pl.BlockSpec((n_kv, GROUP, BLOCK, HEAD_DIM), lambda b, j: (0, 0, b * n_blocks + j, 0)),
                  pl.BlockSpec((seq, width), lambda b, j: (b, 0)),
                  pl.BlockSpec((seq, width), lambda b, j: (b, 0)),
                  pl.BlockSpec((None, L, width), lambda b, j: (b, 0, 0)),
                  pl.BlockSpec((None, L, width), lambda b, j: (b, 0, 0))],
        out_specs=pl.BlockSpec((BLOCK, GROUP * width), lambda b, j: (b * n_blocks + j, 0)),
        out_shape=jax.ShapeDtypeStruct((M, GROUP * width), BF16),
        compiler_params=_params("arbitrary", "arbitrary"),
        name="attn_a_lat",
    )(sink, q, k, v, ck, cv)


def _attn_b_kernel(*refs, has_ctx, lam_init):
    if has_ctx:
        lam_ref, subg_ref, q_ref, k_ref, v_ref, ck_ref, cv_ref, o_ref = refs
    else:
        lam_ref, subg_ref, q_ref, k_ref, v_ref, o_ref = refs
    lv = lam_ref[...]
    lam = (jnp.exp(jnp.sum(lv[0:1] * lv[1:2], axis=-1, keepdims=True))
           - jnp.exp(jnp.sum(lv[2:3] * lv[3:4], axis=-1, keepdims=True)) + lam_init)

    k = k_ref[...].astype(F32)
    v = v_ref[...].astype(BF16)
    if has_ctx:
        k = jnp.concatenate([k, ck_ref[...]], axis=0)
        v = jnp.concatenate([v, cv_ref[...].astype(BF16)], axis=0)
    vx = _with_ones(v)
    lo = _lane() < DIFF_DIM
    k0 = jnp.where(lo, k, 0.0).astype(BF16)
    k1 = jnp.where(lo, 0.0, k).astype(BF16)

    for g in range(GROUP):
        q = q_ref[g]
        maps = []
        for kc in (k0, k1):
            s = _qk(q, kc)
            ox = jnp.dot(_weights(s, jnp.max(s, axis=-1, keepdims=True)), vx, preferred_element_type=F32)
            maps.append(ox[:, :HEAD_DIM] / ox[:, HEAD_DIM:])
        o = maps[0] - lam * maps[1]
        ms = jnp.mean(o * o, axis=-1, keepdims=True)
        o = o * lax.rsqrt(ms + EPS) * subg_ref[...] * (1.0 - lam_init)
        o_ref[:, g * HEAD_DIM:(g + 1) * HEAD_DIM] = o.astype(o_ref.dtype)


def _attn_b(q, k, v, ctx, lam_vecs, sub_g, lam_init, batch, seq, tq):
    n_kv, _, M, _ = q.shape
    n_q = seq // tq
    has_ctx = ctx is not None
    in_specs = [pl.BlockSpec((4, DIFF_DIM), lambda b, h, j: (0, 0)),
                pl.BlockSpec((1, HEAD_DIM), lambda b, h, j: (0, 0)),
                pl.BlockSpec((None, GROUP, tq, HEAD_DIM), lambda b, h, j: (h, 0, b * n_q + j, 0)),
                pl.BlockSpec((seq, HEAD_DIM), lambda b, h, j: (b, h)),
                pl.BlockSpec((seq, HEAD_DIM), lambda b, h, j: (b, h))]
    args = [lam_vecs, sub_g, q, k, v]
    if has_ctx:
        L = ctx[0].shape[1]
        in_specs += [pl.BlockSpec((None, L, HEAD_DIM), lambda b, h, j: (b, 0, h))] * 2
        args += list(ctx)
    return pl.pallas_call(
        functools.partial(_attn_b_kernel, has_ctx=has_ctx, lam_init=lam_init),
        grid=(batch, n_kv, n_q),
        in_specs=in_specs,
        out_specs=pl.BlockSpec((tq, GROUP * HEAD_DIM), lambda b, h, j: (b * n_q + j, h)),
        out_shape=jax.ShapeDtypeStruct((M, n_kv * GROUP * HEAD_DIM), BF16),
        compiler_params=_params("arbitrary", "arbitrary", "arbitrary"),
        name="attn_b",
    )(*args)


def _oproj_kernel(oa_ref, ob_ref, wa_ref, wb_ref, x_ref, gt_ref, y_ref):
    acc = jnp.dot(oa_ref[...], wa_ref[...], preferred_element_type=F32)
    acc = acc + jnp.dot(ob_ref[...], wb_ref[...], preferred_element_type=F32)
    y_ref[...] = x_ref[...] + gt_ref[...] * acc


def _out_proj(o_a, o_b, w_out, x, mod, gt_idx, rows_per_mod):
    M, D = x.shape
    wa = o_a.shape[1]
    wb = o_b.shape[1]
    assert wa == wb and wa + wb == w_out.shape[0]
    tm = _tile(rows_per_mod, 1024) if mod.shape[0] > 1 else _tile(M, 1024)
    tn = _tile(D, 512)
    return pl.pallas_call(
        _oproj_kernel,
        grid=(M // tm, D // tn),
        in_specs=[pl.BlockSpec((tm, wa), lambda m, n: (m, 0)),
                  pl.BlockSpec((tm, wb), lambda m, n: (m, 0)),
                  pl.BlockSpec((wa, tn), lambda m, n: (0, n)),
                  pl.BlockSpec((wb, tn), lambda m, n: (1, n)),
                  pl.BlockSpec((tm, tn), lambda m, n: (m, n)),
                  _mod_spec(mod, gt_idx, tn, tm, rows_per_mod, 0)],
        out_specs=pl.BlockSpec((tm, tn), lambda m, n: (m, n)),
        out_shape=jax.ShapeDtypeStruct((M, D), F32),
        compiler_params=_params("arbitrary", "arbitrary"),
        name="out_proj",
    )(o_a, o_b, w_out, w_out, x, mod)


def _fc1_kernel(a_ref, w_ref, o_ref):
    acc = jnp.dot(a_ref[...], w_ref[...], preferred_element_type=F32)
    r = jnp.maximum(acc, 0.0)
    o_ref[...] = (r * r).astype(o_ref.dtype)


def _fc1(h, w):
    M, D = h.shape
    N = w.shape[1]
    tm = _tile(M, 1024)
    tn = _tile(N, 1024)
    return pl.pallas_call(
        _fc1_kernel,
        grid=(M // tm, N // tn),
        in_specs=[pl.BlockSpec((tm, D), lambda m, n: (m, 0)),
                  pl.BlockSpec((D, tn), lambda m, n: (0, n))],
        out_specs=pl.BlockSpec((tm, tn), lambda m, n: (m, n)),
        out_shape=jax.ShapeDtypeStruct((M, N), BF16),
        compiler_params=_params("arbitrary", "arbitrary"),
        name="fc1",
    )(h, w)


def _fc2_kernel(a_ref, w_ref, x_ref, gt_ref, y_ref, acc_ref):
    k = pl.program_id(2)
    last = pl.num_programs(2) - 1

    def part():
        return jnp.dot(a_ref[...], w_ref[...], preferred_element_type=F32)

    @pl.when(k == 0)
    def _():
        acc_ref[...] = part()

    @pl.when(jnp.logical_and(k > 0, k < last))
    def _():
        acc_ref[...] += part()

    @pl.when(k == last)
    def _():
        y_ref[...] = x_ref[...] + gt_ref[...] * (acc_ref[...] + part())


def _fc2(hid, w, x, mod, gt_idx, rows_per_mod):
    M, D = x.shape
    K = hid.shape[1]
    tm = _tile(rows_per_mod, 1024) if mod.shape[0] > 1 else _tile(M, 1024)
    tn = _tile(D, 1024)
    tk = _tile(K, 2048)
    assert K // tk >= 2
    return pl.pallas_call(
        _fc2_kernel,
        grid=(M // tm, D // tn, K // tk),
        in_specs=[pl.BlockSpec((tm, tk), lambda m, n, k: (m, k)),
                  pl.BlockSpec((tk, tn), lambda m, n, k: (k, n)),
                  pl.BlockSpec((tm, tn), lambda m, n, k: (m, n)),
                  _mod_spec(mod, gt_idx, tn, tm, rows_per_mod, 0)],
        out_specs=pl.BlockSpec((tm, tn), lambda m, n, k: (m, n)),
        out_shape=jax.ShapeDtypeStruct((M, D), F32),
        scratch_shapes=[pltpu.VMEM((tm, tn), F32)],
        compiler_params=_params("arbitrary", "arbitrary", "arbitrary"),
        name="fc2",
    )(hid, w, x, mod)


def _rope_tables(seq, chunk):
    n = chunk // 4
    pos = jnp.arange(seq)
    row = (pos // GRID_W).astype(F32)
    col = (pos % GRID_W).astype(F32)
    freq = ROPE_BASE ** (-jnp.arange(n, dtype=F32) / n)
    ra = row[:, None] * freq
    ca = col[:, None] * freq
    cos = jnp.concatenate([jnp.cos(ra)] * 2 + [jnp.cos(ca)] * 2, axis=-1)
    sin = jnp.concatenate([-jnp.sin(ra), jnp.sin(ra), -jnp.sin(ca), jnp.sin(ca)], axis=-1)
    reps = HEAD_DIM // chunk
    return jnp.tile(cos, (1, reps)), jnp.tile(sin, (1, reps))


def _layer(x, mod, w, l, batch, seq, n_kv_a, n_kv_b, ctx_kv):
    latent = ctx_kv is not None
    wa = n_kv_a * GROUP * HEAD_DIM
    ka = n_kv_a * HEAD_DIM
    wb = n_kv_b * GROUP * HEAD_DIM
    tab_a = _rope_tables(seq, HEAD_DIM) if latent else None
    tab_b = _rope_tables(seq, DIFF_DIM) if latent else None
    kv_dtype = BF16 if latent else F32

    h = _prenorm(x, w["norm1_g"], mod, 0, 1, seq)
    gq_a = w["qnorm_a_g"].reshape(1, HEAD_DIM)
    gk_a = w["knorm_a_g"].reshape(1, HEAD_DIM)
    gq_b = jnp.tile(w["qnorm_b_g"], 2).reshape(1, HEAD_DIM)
    gk_b = jnp.tile(w["knorm_b_g"], 2).reshape(1, HEAD_DIM)
    q_a = _q_proj(h, w["w_in"], 0, n_kv_a, gq_a, HEAD_DIM, tab_a, seq)
    k_a, v_a = _kv_proj(h, w["w_in"], wa, n_kv_a, gk_a, HEAD_DIM, tab_a, seq, kv_dtype)
    q_b = _q_proj(h, w["w_in"], wa + 2 * ka, n_kv_b, gq_b, DIFF_DIM, tab_b, seq)
    k_b, v_b = _kv_proj(h, w["w_in"], wa + 2 * ka + wb, n_kv_b, gk_b, DIFF_DIM, tab_b, seq, kv_dtype)

    lam_vecs = jnp.stack([w["lam_q1"], w["lam_k1"], w["lam_q2"], w["lam_k2"]]).astype(F32)
    lam_init = 0.8 - 0.6 * math.exp(-0.3 * l)
    sub_g = w["subln_g"].reshape(1, HEAD_DIM)
    if latent:
        ck_a, cv_a, ck_b, cv_b = ctx_kv
        o_a = _attn_a_lat(q_a, k_a, v_a, ck_a, cv_a, w["sink_a"], batch, seq)
        o_b = _attn_b(q_b, k_b, v_b, (ck_b, cv_b), lam_vecs, sub_g, lam_init, batch, seq, _tile(seq, 256))
    else:
        o_a = _attn_a_ctx(q_a, k_a, v_a, w["sink_a"], batch, seq)
        o_b = _attn_b(q_b, k_b, v_b, None, lam_vecs, sub_g, lam_init, batch, seq, _tile(seq, 256))

    x = _out_proj(o_a, o_b, w["w_out"], x, mod, 2, seq)
    h2 = _prenorm(x, w["norm2_g"], mod, 3, 4, seq)
    hid = _fc1(h2, w["w_fc1"])
    x = _fc2(hid, w["w_fc2"], x, mod, 5, seq)
    return x, (k_a, v_a, k_b, v_b)


def kernel(x_prompt, x_sample, cache_a_k, cache_a_v, cache_b_k, cache_b_v, c, c_ctx, w_mod, b_mod, norm1_g, w_in, qnorm_a_g, knorm_a_g, qnorm_b_g, knorm_b_g, sink_a, lam_q1, lam_k1, lam_q2, lam_k2, subln_g, w_out, norm2_g, w_fc1, w_fc2):
    batch, seq, D = x_prompt.shape
    dbatch, dseq, _ = x_sample.shape
    depth = w_mod.shape[0]
    past = cache_a_k.shape[2]
    n_kv_a = cache_a_k.shape[3]
    n_kv_b = cache_b_k.shape[3]
    assert dseq % GRID_W == 0 and dseq % BLOCK == 0

    per_layer = dict(norm1_g=norm1_g, qnorm_a_g=qnorm_a_g, knorm_a_g=knorm_a_g, qnorm_b_g=qnorm_b_g,
                     knorm_b_g=knorm_b_g, sink_a=sink_a, lam_q1=lam_q1, lam_k1=lam_k1, lam_q2=lam_q2,
                     lam_k2=lam_k2, subln_g=subln_g, norm2_g=norm2_g)
    big = dict(w_in=w_in, w_out=w_out, w_fc1=w_fc1, w_fc2=w_fc2)

    n_rows = -(-(dbatch + 1) // 8) * 8
    cvecs = jnp.zeros((n_rows, D), F32).at[:dbatch].set(c).at[dbatch].set(c_ctx)

    xp = x_prompt.reshape(batch * seq, D)
    xs = x_sample.reshape(dbatch * dseq, D)
    new_kv = []
    for l in range(depth):
        w = {k: v[l] for k, v in per_layer.items()}
        w.update({k: v[l].astype(BF16) for k, v in big.items()})
        mod = _modulation(cvecs, w_mod[l], b_mod[l])
        mod_ctx = mod[dbatch:dbatch + 1].reshape(1, 1, 6 * D)
        mod_lat = mod[:dbatch].reshape(dbatch, 1, 6 * D)

        xp, kv = _layer(xp, mod_ctx, w, l, batch, seq, n_kv_a, n_kv_b, None)
        new_kv.append(kv)
        ctx_kv = (cache_a_k[:, l].reshape(dbatch, past, n_kv_a * HEAD_DIM),
                  cache_a_v[:, l].reshape(dbatch, past, n_kv_a * HEAD_DIM),
                  cache_b_k[:, l].reshape(dbatch, past, n_kv_b * HEAD_DIM),
                  cache_b_v[:, l].reshape(dbatch, past, n_kv_b * HEAD_DIM))
        xs, _ = _layer(xs, mod_lat, w, l, dbatch, dseq, n_kv_a, n_kv_b, ctx_kv)

    def stack(i, shape):
        return jnp.stack([kv[i].reshape((batch, seq) + shape) for kv in new_kv], axis=1)

    return (xp.reshape(batch, seq, D), xs.reshape(dbatch, dseq, D),
            stack(0, (n_kv_a, HEAD_DIM)), stack(1, (n_kv_a, HEAD_DIM)),
            stack(2, (n_kv_b, 2, DIFF_DIM)), stack(3, (n_kv_b, HEAD_DIM)))
```

```python
import functools
import math

import jax
import jax.numpy as jnp
from jax import lax
from jax.experimental import pallas as pl
from jax.experimental.pallas import tpu as pltpu

F32 = jnp.float32
BF16 = jnp.bfloat16

HEAD_DIM = 128
BF16_SUBLANES = 16
DIFF_DIM = HEAD_DIM // 2
GROUP = 4
GRID_W = 64
WINDOW = 128
BLOCK = 128
ROPE_BASE = 10000.0
EPS = 1e-6
NEG = -1e30
LOG2E = math.log2(math.e)
V7X_VMEM_LIMIT_BYTES = 56 * 1024 * 1024


def _params(*sem):
    return pltpu.CompilerParams(dimension_semantics=sem, vmem_limit_bytes=V7X_VMEM_LIMIT_BYTES)


def _tile(dim, want):
    t = min(dim, want)
    assert dim % t == 0, (dim, want)
    return t


def _mod_kernel(c_ref, w_ref, b_ref, o_ref):
    c = c_ref[...]
    a = (c * jax.nn.sigmoid(c)).astype(BF16)
    acc = jnp.dot(a, w_ref[...].astype(BF16), preferred_element_type=F32)
    o_ref[...] = acc + b_ref[...]


def _modulation(cvecs, w_mod, b_mod):
    R, D = cvecs.shape
    N = w_mod.shape[1]
    tn = _tile(N, 512)
    return pl.pallas_call(
        _mod_kernel,
        grid=(N // tn,),
        in_specs=[pl.BlockSpec((R, D), lambda n: (0, 0)),
                  pl.BlockSpec((D, tn), lambda n: (0, n)),
                  pl.BlockSpec((1, tn), lambda n: (0, n))],
        out_specs=pl.BlockSpec((R, tn), lambda n: (0, n)),
        out_shape=jax.ShapeDtypeStruct((R, N), F32),
        compiler_params=_params("arbitrary"),
        name="modulation",
    )(cvecs, w_mod, b_mod.reshape(1, N))


def _mod_spec(mod, which, tn, tm, rows_per_mod, n_axis):
    per_batch = mod.shape[0] > 1
    D = mod.shape[2] // 6
    off = which * (D // tn)

    def index(m, *rest):
        n = rest[n_axis] if n_axis is not None else 0
        return ((m * tm) // rows_per_mod if per_batch else 0, 0, off + n)

    return pl.BlockSpec((None, 1, tn), index)


def _prenorm_kernel(x_ref, g_ref, sh_ref, sc_ref, o_ref):
    x = x_ref[...]
    ms = jnp.mean(x * x, axis=-1, keepdims=True)
    y = x * lax.rsqrt(ms + EPS) * g_ref[...]
    o_ref[...] = (y * (1.0 + sc_ref[...]) + sh_ref[...]).astype(o_ref.dtype)


def _prenorm(x, g, mod, sh_idx, sc_idx, rows_per_mod):
    M, D = x.shape
    tm = _tile(M, 256)
    assert rows_per_mod % tm == 0
    return pl.pallas_call(
        _prenorm_kernel,
        grid=(M // tm,),
        in_specs=[pl.BlockSpec((tm, D), lambda m: (m, 0)),
                  pl.BlockSpec((1, D), lambda m: (0, 0)),
                  _mod_spec(mod, sh_idx, D, tm, rows_per_mod, None),
                  _mod_spec(mod, sc_idx, D, tm, rows_per_mod, None)],
        out_specs=pl.BlockSpec((tm, D), lambda m: (m, 0)),
        out_shape=jax.ShapeDtypeStruct((M, D), BF16),
        compiler_params=_params("arbitrary"),
        name="prenorm",
    )(x, g.reshape(1, D), mod, mod)


def _lane():
    return lax.broadcasted_iota(jnp.int32, (1, HEAD_DIM), 1)


def _head_norm(y, gain, chunk):
    y2 = y * y
    if chunk == HEAD_DIM:
        ms = jnp.mean(y2, axis=-1, keepdims=True)
    else:
        lo = _lane() < chunk
        s0 = jnp.sum(jnp.where(lo, y2, 0.0), axis=-1, keepdims=True)
        s1 = jnp.sum(jnp.where(lo, 0.0, y2), axis=-1, keepdims=True)
        ms = jnp.where(lo, s0, s1) * (1.0 / chunk)
    return y * lax.rsqrt(ms + EPS) * gain


def _rope(y, cos, sin, chunk):
    rot = chunk // 4
    first = (_lane() % (2 * rot)) < rot
    partner = jnp.where(first, pltpu.roll(y, HEAD_DIM - rot, 1), pltpu.roll(y, rot, 1))
    return y * cos + partner * sin


PROJ_ROW_CHUNK = 128


def _row_chunks(rows):
    step = min(rows, PROJ_ROW_CHUNK)
    return [slice(r, r + step) for r in range(0, rows, step)]


def _qproj_kernel(*refs, chunk, scale, rope):
    if rope:
        a_ref, w_ref, g_ref, cos_ref, sin_ref, o_ref = refs
    else:
        a_ref, w_ref, g_ref, o_ref = refs
    for rows in _row_chunks(a_ref.shape[0]):
        acc = jnp.dot(a_ref[rows, :], w_ref[...], preferred_element_type=F32)
        for h in range(GROUP):
            y = _head_norm(acc[:, h * HEAD_DIM:(h + 1) * HEAD_DIM], g_ref[...], chunk)
            if rope:
                y = _rope(y, cos_ref[rows, :], sin_ref[rows, :], chunk)
            o_ref[h, rows, :] = (y * scale).astype(o_ref.dtype)


def _q_proj(h, w_in, col0, n_kv, gain, chunk, tables, seq):
    M, D = h.shape
    tm = _tile(seq, 1024) if tables is not None else _tile(M, 1024)
    tn = GROUP * HEAD_DIM
    assert col0 % tn == 0
    rope = tables is not None
    in_specs = [pl.BlockSpec((tm, D), lambda m, n: (m, 0)),
                pl.BlockSpec((D, tn), lambda m, n: (0, col0 // tn + n)),
                pl.BlockSpec((1, HEAD_DIM), lambda m, n: (0, 0))]
    args = [h, w_in, gain]
    if rope:
        per_seq = seq // tm
        in_specs += [pl.BlockSpec((tm, HEAD_DIM), lambda m, n: (m % per_seq, 0))] * 2
        args += list(tables)
    return pl.pallas_call(
        functools.partial(_qproj_kernel, chunk=chunk, scale=chunk ** -0.5 * LOG2E, rope=rope),
        grid=(M // tm, n_kv),
        in_specs=in_specs,
        out_specs=pl.BlockSpec((None, GROUP, tm, HEAD_DIM), lambda m, n: (n, 0, m, 0)),
        out_shape=jax.ShapeDtypeStruct((n_kv, GROUP, M, HEAD_DIM), BF16),
        compiler_params=_params("arbitrary", "arbitrary"),
        name="q_proj",
    )(*args)


def _kvproj_kernel(*refs, chunk, rope, n_kv):
    if rope:
        a_ref, w_ref, g_ref, cos_ref, sin_ref, k_ref, v_ref = refs
    else:
        a_ref, w_ref, g_ref, k_ref, v_ref = refs
    width = n_kv * HEAD_DIM
    for rows in _row_chunks(a_ref.shape[0]):
        acc = jnp.dot(a_ref[rows, :], w_ref[...], preferred_element_type=F32)
        for h in range(n_kv):
            sl = slice(h * HEAD_DIM, (h + 1) * HEAD_DIM)
            y = _head_norm(acc[:, sl], g_ref[...], chunk)
            if rope:
                y = _rope(y, cos_ref[rows, :], sin_ref[rows, :], chunk)
            k_ref[rows, sl] = y.astype(k_ref.dtype)
        v_ref[rows, :] = acc[:, width:].astype(v_ref.dtype)


def _kv_proj(h, w_in, col0, n_kv, gain, chunk, tables, seq, out_dtype):
    M, D = h.shape
    tm = _tile(seq, 1024) if tables is not None else _tile(M, 1024)
    width = n_kv * HEAD_DIM
    tn = 2 * width
    assert col0 % tn == 0
    rope = tables is not None
    in_specs = [pl.BlockSpec((tm, D), lambda m: (m, 0)),
                pl.BlockSpec((D, tn), lambda m: (0, col0 // tn)),
                pl.BlockSpec((1, HEAD_DIM), lambda m: (0, 0))]
    args = [h, w_in, gain]
    if rope:
        per_seq = seq // tm
        in_specs += [pl.BlockSpec((tm, HEAD_DIM), lambda m: (m % per_seq, 0))] * 2
        args += list(tables)
    return pl.pallas_call(
        functools.partial(_kvproj_kernel, chunk=chunk, rope=rope, n_kv=n_kv),
        grid=(M // tm,),
        in_specs=in_specs,
        out_specs=[pl.BlockSpec((tm, width), lambda m: (m, 0))] * 2,
        out_shape=[jax.ShapeDtypeStruct((M, width), out_dtype)] * 2,
        compiler_params=_params("arbitrary"),
        name="kv_proj",
    )(*args)


def _with_ones(v):
    return jnp.concatenate([v, jnp.ones_like(v)], axis=1)


def _weights(s, m):
    return jnp.exp2((s - m).astype(BF16))


def _softmax_sink_pv(scores, values_ones, sink):
    m = sink
    for s in scores:
        m = jnp.maximum(m, jnp.max(s, axis=-1, keepdims=True))
    ox = None
    for s, vx in zip(scores, values_ones):
        pv = jnp.dot(_weights(s, m), vx, preferred_element_type=F32)
        ox = pv if ox is None else ox + pv
    denom = ox[:, HEAD_DIM:] + jnp.exp2(sink - m)
    return ox[:, :HEAD_DIM] / denom


def _qk(q, k):
    return lax.dot_general(q, k, (((1,), (1,)), ((), ())), preferred_element_type=F32)


def _attn_a_ctx_kernel(sink_ref, q_ref, k_ref, v_ref, o_ref, *, n_kv):
    for h in range(n_kv):
        cols = slice(h * HEAD_DIM, (h + 1) * HEAD_DIM)
        k = k_ref[:, cols].astype(BF16)
        vx = _with_ones(v_ref[:, cols].astype(BF16))
        for g in range(GROUP):
            head = h * GROUP + g
            sink = jnp.full((1, 1), sink_ref[head] * LOG2E, F32)
            o = _softmax_sink_pv([_qk(q_ref[h, g], k)], [vx], sink)
            o_ref[:, head * HEAD_DIM:(head + 1) * HEAD_DIM] = o.astype(o_ref.dtype)


def _attn_a_ctx(q, k, v, sink, batch, seq):
    n_kv, _, M, _ = q.shape
    width = n_kv * HEAD_DIM
    return pl.pallas_call(
        functools.partial(_attn_a_ctx_kernel, n_kv=n_kv),
        grid=(batch,),
        in_specs=[pl.BlockSpec(memory_space=pltpu.SMEM),
                  pl.BlockSpec((n_kv, GROUP, seq, HEAD_DIM), lambda b: (0, 0, b, 0)),
                  pl.BlockSpec((seq, width), lambda b: (b, 0)),
                  pl.BlockSpec((seq, width), lambda b: (b, 0))],
        out_specs=pl.BlockSpec((seq, GROUP * width), lambda b: (b, 0)),
        out_shape=jax.ShapeDtypeStruct((M, GROUP * width), BF16),
        compiler_params=_params("arbitrary"),
        name="attn_a_ctx",
    )(sink, q, k, v)


def _attn_a_lat_kernel(sink_ref, q_ref, k_ref, v_ref, ck_ref, cv_ref, o_ref, *, n_blocks, n_kv):
    j = pl.program_id(1)
    rows = GROUP * BLOCK
    span = 3 * BLOCK
    start = pl.multiple_of(jnp.clip(j - 1, 0, n_blocks - 3) * BLOCK, BLOCK)
    row = lax.broadcasted_iota(jnp.int32, (rows, 1), 0)
    qpos = j * BLOCK + row % BLOCK
    kpos = start + lax.broadcasted_iota(jnp.int32, (1, span), 1)
    in_window = jnp.abs(kpos - qpos) <= WINDOW

    for h in range(n_kv):
        cols = slice(h * HEAD_DIM, (h + 1) * HEAD_DIM)
        kw = k_ref[pl.ds(start, span), cols]
        vw = _with_ones(v_ref[pl.ds(start, span), cols])
        kc = ck_ref[:, cols].astype(BF16)
        vc = _with_ones(cv_ref[:, cols].astype(BF16))
        q = q_ref[h].reshape(rows, HEAD_DIM)
        s_w = jnp.where(in_window, _qk(q, kw), NEG)
        s_c = _qk(q, kc)
        sink = jnp.zeros((rows, 1), F32)
        for g in range(GROUP):
            sink = jnp.where(row // BLOCK == g, sink_ref[h * GROUP + g] * LOG2E, sink)
        o = _softmax_sink_pv([s_w, s_c], [vw, vc], sink)
        for g in range(GROUP):
            head = h * GROUP + g
            o_ref[:, head * HEAD_DIM:(head + 1) * HEAD_DIM] = o[g * BLOCK:(g + 1) * BLOCK].astype(o_ref.dtype)


def _attn_a_lat(q, k, v, ck, cv, sink, batch, seq):
    n_kv, _, M, _ = q.shape
    n_blocks = seq // BLOCK
    assert n_blocks >= 3
    L = ck.shape[1]
    width = n_kv * HEAD_DIM
    return pl.pallas_call(
        functools.partial(_attn_a_lat_kernel, n_blocks=n_blocks, n_kv=n_kv),
        grid=(batch, n_blocks),
        in_specs=[pl.BlockSpec(memory_space=pltpu.SMEM),
                  pl.BlockSpec((n_kv, GROUP, BLOCK, HEAD_DIM), lambda b, j: (0, 0, b * n_blocks + j, 0)),
                  pl.BlockSpec((seq, width), lambda b, j: (b, 0)),
                  pl.BlockSpec((seq, width), lambda b, j: (b, 0)),
                  pl.BlockSpec((None, L, width), lambda b, j: (b, 0, 0)),
                  pl.BlockSpec((None, L, width), lambda b, j: (b, 0, 0))],
        out_specs=pl.BlockSpec((BLOCK, GROUP * width), lambda b, j: (b * n_blocks + j, 0)),
        out_shape=jax.ShapeDtypeStruct((M, GROUP * width), BF16),
        compiler_params=_params("arbitrary", "arbitrary"),
        name="attn_a_lat",
    )(sink, q, k, v, ck, cv)


def _attn_b_kernel(*refs, has_ctx, lam_init, n_cast):
    n_in = 7 if has_ctx else 5
    if has_ctx:
        lam_ref, subg_ref, q_ref, k_ref, v_ref, ck_ref, cv_ref = refs[:n_in]
    else:
        lam_ref, subg_ref, q_ref, k_ref, v_ref = refs[:n_in]
    o_ref = refs[n_in + n_cast]
    for src, dst in zip(refs[n_in:n_in + n_cast], refs[n_in + n_cast + 1:]):
        dst[...] = src[...].astype(dst.dtype)

    lv = lam_ref[...]
    lam = (jnp.exp(jnp.sum(lv[0:1] * lv[1:2], axis=-1, keepdims=True))
           - jnp.exp(jnp.sum(lv[2:3] * lv[3:4], axis=-1, keepdims=True)) + lam_init)

    k = k_ref[...].astype(F32)
    v = v_ref[...].astype(BF16)
    if has_ctx:
        k = jnp.concatenate([k, ck_ref[...]], axis=0)
        v = jnp.concatenate([v, cv_ref[...].astype(BF16)], axis=0)
    vx = _with_ones(v)
    lo = _lane() < DIFF_DIM
    k0 = jnp.where(lo, k, 0.0).astype(BF16)
    k1 = jnp.where(lo, 0.0, k).astype(BF16)

    for g in range(GROUP):
        q = q_ref[g]
        maps = []
        for kc in (k0, k1):
            s = _qk(q, kc)
            ox = jnp.dot(_weights(s, jnp.max(s, axis=-1, keepdims=True)), vx, preferred_element_type=F32)
            maps.append(ox[:, :HEAD_DIM] / ox[:, HEAD_DIM:])
        o = maps[0] - lam * maps[1]
        ms = jnp.mean(o * o, axis=-1, keepdims=True)
        o = o * lax.rsqrt(ms + EPS) * subg_ref[...] * (1.0 - lam_init)
        o_ref[:, g * HEAD_DIM:(g + 1) * HEAD_DIM] = o.astype(o_ref.dtype)


def _attn_b(q, k, v, ctx, lam_vecs, sub_g, lam_init, batch, seq, tq, casts=()):
    n_kv, _, M, _ = q.shape
    n_q = seq // tq
    has_ctx = ctx is not None
    n_steps = batch * n_kv * n_q
    in_specs = [pl.BlockSpec((4, DIFF_DIM), lambda b, h, j: (0, 0)),
                pl.BlockSpec((1, HEAD_DIM), lambda b, h, j: (0, 0)),
                pl.BlockSpec((None, GROUP, tq, HEAD_DIM), lambda b, h, j: (h, 0, b * n_q + j, 0)),
                pl.BlockSpec((seq, HEAD_DIM), lambda b, h, j: (b, h)),
                pl.BlockSpec((seq, HEAD_DIM), lambda b, h, j: (b, h))]
    args = [lam_vecs, sub_g, q, k, v]
    if has_ctx:
        L = ctx[0].shape[1]
        in_specs += [pl.BlockSpec((None, L, HEAD_DIM), lambda b, h, j: (b, 0, h))] * 2
        args += list(ctx)
    out_specs = [pl.BlockSpec((tq, GROUP * HEAD_DIM), lambda b, h, j: (b * n_q + j, h))]
    out_shape = [jax.ShapeDtypeStruct((M, n_kv * GROUP * HEAD_DIM), BF16)]
    for w in casts:
        rows, cols = w.shape
        slab = rows // n_steps
        assert rows % n_steps == 0 and slab % BF16_SUBLANES == 0, (rows, n_steps)
        spec = pl.BlockSpec((slab, cols), lambda b, h, j: ((b * n_kv + h) * n_q + j, 0))
        in_specs.append(spec)
        args.append(w)
        out_specs.append(spec)
        out_shape.append(jax.ShapeDtypeStruct((rows, cols), BF16))
    out = pl.pallas_call(
        functools.partial(_attn_b_kernel, has_ctx=has_ctx, lam_init=lam_init, n_cast=len(casts)),
        grid=(batch, n_kv, n_q),
        in_specs=in_specs,
        out_specs=out_specs,
        out_shape=out_shape,
        compiler_params=_params("arbitrary", "arbitrary", "arbitrary"),
        name="attn_b",
    )(*args)
    return out[0], list(out[1:])


def _oproj_kernel(oa_ref, ob_ref, wa_ref, wb_ref, x_ref, gt_ref, y_ref):
    acc = jnp.dot(oa_ref[...], wa_ref[...], preferred_element_type=F32)
    acc = acc + jnp.dot(ob_ref[...], wb_ref[...], preferred_element_type=F32)
    y_ref[...] = x_ref[...] + gt_ref[...] * acc


def _out_proj(o_a, o_b, w_out, x, mod, gt_idx, rows_per_mod):
    M, D = x.shape
    wa = o_a.shape[1]
    wb = o_b.shape[1]
    assert wa == wb and wa + wb == w_out.shape[0]
    tm = _tile(rows_per_mod, 1024) if mod.shape[0] > 1 else _tile(M, 1024)
    tn = _tile(D, 1024)
    return pl.pallas_call(
        _oproj_kernel,
        grid=(M // tm, D // tn),
        in_specs=[pl.BlockSpec((tm, wa), lambda m, n: (m, 0)),
                  pl.BlockSpec((tm, wb), lambda m, n: (m, 0)),
                  pl.BlockSpec((wa, tn), lambda m, n: (0, n)),
                  pl.BlockSpec((wb, tn), lambda m, n: (1, n)),
                  pl.BlockSpec((tm, tn), lambda m, n: (m, n)),
                  _mod_spec(mod, gt_idx, tn, tm, rows_per_mod, 0)],
        out_specs=pl.BlockSpec((tm, tn), lambda m, n: (m, n)),
        out_shape=jax.ShapeDtypeStruct((M, D), F32),
        compiler_params=_params("arbitrary", "arbitrary"),
        name="out_proj",
    )(o_a, o_b, w_out, w_out, x, mod)


def _fc1_kernel(a_ref, w_ref, o_ref):
    acc = jnp.dot(a_ref[...], w_ref[...], preferred_element_type=F32)
    r = jnp.maximum(acc, 0.0)
    o_ref[...] = (r * r).astype(o_ref.dtype)


def _fc1(h, w):
    M, D = h.shape
    N = w.shape[1]
    tm = _tile(M, 1024)
    tn = _tile(N, 1024)
    return pl.pallas_call(
        _fc1_kernel,
        grid=(M // tm, N // tn),
        in_specs=[pl.BlockSpec((tm, D), lambda m, n: (m, 0)),
                  pl.BlockSpec((D, tn), lambda m, n: (0, n))],
        out_specs=pl.BlockSpec((tm, tn), lambda m, n: (m, n)),
        out_shape=jax.ShapeDtypeStruct((M, N), BF16),
        compiler_params=_params("arbitrary", "arbitrary"),
        name="fc1",
    )(h, w)


def _fc2_kernel(a_ref, w_ref, x_ref, gt_ref, y_ref):
    k = pl.program_id(2)
    last = pl.num_programs(2) - 1

    def part():
        return jnp.dot(a_ref[...], w_ref[...], preferred_element_type=F32)

    @pl.when(k == 0)
    def _():
        y_ref[...] = part()

    @pl.when(jnp.logical_and(k > 0, k < last))
    def _():
        y_ref[...] += part()

    @pl.when(k == last)
    def _():
        y_ref[...] = x_ref[...] + gt_ref[...] * (y_ref[...] + part())


def _fc2(hid, w, x, mod, gt_idx, rows_per_mod):
    M, D = x.shape
    K = hid.shape[1]
    tm = _tile(rows_per_mod, 1024) if mod.shape[0] > 1 else _tile(M, 1024)
    tn = _tile(D, 2048)
    tk = _tile(K, 1024)
    assert K // tk >= 2
    return pl.pallas_call(
        _fc2_kernel,
        grid=(M // tm, D // tn, K // tk),
        in_specs=[pl.BlockSpec((tm, tk), lambda m, n, k: (m, k)),
                  pl.BlockSpec((tk, tn), lambda m, n, k: (k, n)),
                  pl.BlockSpec((tm, tn), lambda m, n, k: (m, n)),
                  _mod_spec(mod, gt_idx, tn, tm, rows_per_mod, 0)],
        out_specs=pl.BlockSpec((tm, tn), lambda m, n, k: (m, n)),
        out_shape=jax.ShapeDtypeStruct((M, D), F32),
        compiler_params=_params("arbitrary", "arbitrary", "arbitrary"),
        name="fc2",
    )(hid, w, x, mod)


def _rope_tables(seq, chunk):
    n = chunk // 4
    pos = jnp.arange(seq)
    row = (pos // GRID_W).astype(F32)
    col = (pos % GRID_W).astype(F32)
    freq = ROPE_BASE ** (-jnp.arange(n, dtype=F32) / n)
    ra = row[:, None] * freq
    ca = col[:, None] * freq
    cos = jnp.concatenate([jnp.cos(ra)] * 2 + [jnp.cos(ca)] * 2, axis=-1)
    sin = jnp.concatenate([-jnp.sin(ra), jnp.sin(ra), -jnp.sin(ca), jnp.sin(ca)], axis=-1)
    reps = HEAD_DIM // chunk
    return jnp.tile(cos, (1, reps)), jnp.tile(sin, (1, reps))


def _mixer(x, mod, w, l, batch, seq, n_kv_a, n_kv_b, ctx_kv, casts=()):
    latent = ctx_kv is not None
    wa = n_kv_a * GROUP * HEAD_DIM
    ka = n_kv_a * HEAD_DIM
    wb = n_kv_b * GROUP * HEAD_DIM
    tab_a = _rope_tables(seq, HEAD_DIM) if latent else None
    tab_b = _rope_tables(seq, DIFF_DIM) if latent else None
    kv_dtype = BF16 if latent else F32

    h = _prenorm(x, w["norm1_g"], mod, 0, 1, seq)
    gq_a = w["qnorm_a_g"].reshape(1, HEAD_DIM)
    gk_a = w["knorm_a_g"].reshape(1, HEAD_DIM)
    gq_b = jnp.tile(w["qnorm_b_g"], 2).reshape(1, HEAD_DIM)
    gk_b = jnp.tile(w["knorm_b_g"], 2).reshape(1, HEAD_DIM)
    q_a = _q_proj(h, w["w_in"], 0, n_kv_a, gq_a, HEAD_DIM, tab_a, seq)
    k_a, v_a = _kv_proj(h, w["w_in"], wa, n_kv_a, gk_a, HEAD_DIM, tab_a, seq, kv_dtype)
    q_b = _q_proj(h, w["w_in"], wa + 2 * ka, n_kv_b, gq_b, DIFF_DIM, tab_b, seq)
    k_b, v_b = _kv_proj(h, w["w_in"], wa + 2 * ka + wb, n_kv_b, gk_b, DIFF_DIM, tab_b, seq, kv_dtype)

    lam_vecs = jnp.stack([w["lam_q1"], w["lam_k1"], w["lam_q2"], w["lam_k2"]]).astype(F32)
    lam_init = 0.8 - 0.6 * math.exp(-0.3 * l)
    sub_g = w["subln_g"].reshape(1, HEAD_DIM)
    if latent:
        ck_a, cv_a, ck_b, cv_b = ctx_kv
        o_a = _attn_a_lat(q_a, k_a, v_a, ck_a, cv_a, w["sink_a"], batch, seq)
        ctx_b = (ck_b, cv_b)
    else:
        o_a = _attn_a_ctx(q_a, k_a, v_a, w["sink_a"], batch, seq)
        ctx_b = None
    o_b, casted = _attn_b(q_b, k_b, v_b, ctx_b, lam_vecs, sub_g, lam_init, batch, seq, _tile(seq, 256), casts)
    x = _out_proj(o_a, o_b, w["w_out"], x, mod, 2, seq)
    return x, (k_a, v_a, k_b, v_b), casted


def _mlp(x, mod, norm_g, w_fc1, w_fc2, seq):
    h2 = _prenorm(x, norm_g, mod, 3, 4, seq)
    hid = _fc1(h2, w_fc1)
    return _fc2(hid, w_fc2, x, mod, 5, seq)


def kernel(x_prompt, x_sample, cache_a_k, cache_a_v, cache_b_k, cache_b_v, c, c_ctx, w_mod, b_mod, norm1_g, w_in, qnorm_a_g, knorm_a_g, qnorm_b_g, knorm_b_g, sink_a, lam_q1, lam_k1, lam_q2, lam_k2, subln_g, w_out, norm2_g, w_fc1, w_fc2):
    batch, seq, D = x_prompt.shape
    dbatch, dseq, _ = x_sample.shape
    depth = w_mod.shape[0]
    past = cache_a_k.shape[2]
    n_kv_a = cache_a_k.shape[3]
    n_kv_b = cache_b_k.shape[3]
    assert dseq % GRID_W == 0 and dseq % BLOCK == 0

    per_layer = dict(norm1_g=norm1_g, qnorm_a_g=qnorm_a_g, knorm_a_g=knorm_a_g, qnorm_b_g=qnorm_b_g,
                     knorm_b_g=knorm_b_g, sink_a=sink_a, lam_q1=lam_q1, lam_k1=lam_k1, lam_q2=lam_q2,
                     lam_k2=lam_k2, subln_g=subln_g, norm2_g=norm2_g)
    big = dict(w_in=w_in, w_out=w_out)

    n_rows = -(-(dbatch + 1) // 8) * 8
    cvecs = jnp.zeros((n_rows, D), F32).at[:dbatch].set(c).at[dbatch].set(c_ctx)

    xp = x_prompt.reshape(batch * seq, D)
    xs = x_sample.reshape(dbatch * dseq, D)
    new_kv = []
    for l in range(depth):
        w = {k: v[l] for k, v in per_layer.items()}
        w.update({k: v[l].astype(BF16) for k, v in big.items()})
        mod = _modulation(cvecs, w_mod[l], b_mod[l])
        mod_ctx = mod[dbatch:dbatch + 1].reshape(1, 1, 6 * D)
        mod_lat = mod[:dbatch].reshape(dbatch, 1, 6 * D)

        xp, kv, _ = _mixer(xp, mod_ctx, w, l, batch, seq, n_kv_a, n_kv_b, None)
        new_kv.append(kv)
        ctx_kv = (cache_a_k[:, l].reshape(dbatch, past, n_kv_a * HEAD_DIM),
                  cache_a_v[:, l].reshape(dbatch, past, n_kv_a * HEAD_DIM),
                  cache_b_k[:, l].reshape(dbatch, past, n_kv_b * HEAD_DIM),
                  cache_b_v[:, l].reshape(dbatch, past, n_kv_b * HEAD_DIM))
        xs, _, (fc1_bf16, fc2_bf16) = _mixer(xs, mod_lat, w, l, dbatch, dseq, n_kv_a, n_kv_b, ctx_kv,
                                             casts=(w_fc1[l], w_fc2[l]))
        xp = _mlp(xp, mod_ctx, w["norm2_g"], fc1_bf16, fc2_bf16, seq)
        xs = _mlp(xs, mod_lat, w["norm2_g"], fc1_bf16, fc2_bf16, dseq)

    def stack(i, shape):
        return jnp.stack([kv[i].reshape((batch, seq) + shape) for kv in new_kv], axis=1)

    return (xp.reshape(batch, seq, D), xs.reshape(dbatch, dseq, D),
            stack(0, (n_kv_a, HEAD_DIM)), stack(1, (n_kv_a, HEAD_DIM)),
            stack(2, (n_kv_b, 2, DIFF_DIM)), stack(3, (n_kv_b, HEAD_DIM)))
```

```python
import functools
import math

import jax
import jax.numpy as jnp
from jax import lax
from jax.experimental import pallas as pl
from jax.experimental.pallas import tpu as pltpu

F32 = jnp.float32
BF16 = jnp.bfloat16

HEAD_DIM = 128
BF16_SUBLANES = 16
DIFF_DIM = HEAD_DIM // 2
GROUP = 4
GRID_W = 64
WINDOW = 128
BLOCK = 128
ROPE_BASE = 10000.0
EPS = 1e-6
NEG = -1e30
LOG2E = math.log2(math.e)
V7X_VMEM_LIMIT_BYTES = 56 * 1024 * 1024


def _params(*sem):
    return pltpu.CompilerParams(dimension_semantics=sem, vmem_limit_bytes=V7X_VMEM_LIMIT_BYTES)


def _tile(dim, want):
    t = min(dim, want)
    assert dim % t == 0, (dim, want)
    return t


def _mod_kernel(c_ref, w_ref, b_ref, o_ref):
    c = c_ref[...]
    a = (c * jax.nn.sigmoid(c)).astype(BF16)
    acc = jnp.dot(a, w_ref[...].astype(BF16), preferred_element_type=F32)
    o_ref[...] = acc + b_ref[...]


def _modulation(cvecs, w_mod, b_mod):
    R, D = cvecs.shape
    N = w_mod.shape[1]
    tn = _tile(N, 512)
    return pl.pallas_call(
        _mod_kernel,
        grid=(N // tn,),
        in_specs=[pl.BlockSpec((R, D), lambda n: (0, 0)),
                  pl.BlockSpec((D, tn), lambda n: (0, n)),
                  pl.BlockSpec((1, tn), lambda n: (0, n))],
        out_specs=pl.BlockSpec((R, tn), lambda n: (0, n)),
        out_shape=jax.ShapeDtypeStruct((R, N), F32),
        compiler_params=_params("arbitrary"),
        name="modulation",
    )(cvecs, w_mod, b_mod.reshape(1, N))


def _mod_spec(mod, which, tn, tm, rows_per_mod, n_axis):
    per_batch = mod.shape[0] > 1
    D = mod.shape[2] // 6
    off = which * (D // tn)

    def index(m, *rest):
        n = rest[n_axis] if n_axis is not None else 0
        return ((m * tm) // rows_per_mod if per_batch else 0, 0, off + n)

    return pl.BlockSpec((None, 1, tn), index)


def _prenorm_kernel(x_ref, g_ref, sh_ref, sc_ref, o_ref):
    x = x_ref[...]
    ms = jnp.mean(x * x, axis=-1, keepdims=True)
    y = x * lax.rsqrt(ms + EPS) * g_ref[...]
    o_ref[...] = (y * (1.0 + sc_ref[...]) + sh_ref[...]).astype(o_ref.dtype)


def _prenorm(x, g, mod, sh_idx, sc_idx, rows_per_mod):
    M, D = x.shape
    tm = _tile(M, 256)
    assert rows_per_mod % tm == 0
    return pl.pallas_call(
        _prenorm_kernel,
        grid=(M // tm,),
        in_specs=[pl.BlockSpec((tm, D), lambda m: (m, 0)),
                  pl.BlockSpec((1, D), lambda m: (0, 0)),
                  _mod_spec(mod, sh_idx, D, tm, rows_per_mod, None),
                  _mod_spec(mod, sc_idx, D, tm, rows_per_mod, None)],
        out_specs=pl.BlockSpec((tm, D), lambda m: (m, 0)),
        out_shape=jax.ShapeDtypeStruct((M, D), BF16),
        compiler_params=_params("arbitrary"),
        name="prenorm",
    )(x, g.reshape(1, D), mod, mod)


class _Riders:
    N_INPUTS = {"cast": 1, "norm": 4}

    def __init__(self, n_steps, step_index):
        self.n_steps = n_steps
        self.step_index = step_index
        self.kinds, self.in_specs, self.args, self.out_specs, self.out_shapes = [], [], [], [], []

    def _slab_spec(self, rows, cols):
        slab = rows // self.n_steps
        assert rows % self.n_steps == 0 and slab % BF16_SUBLANES == 0, (rows, self.n_steps)
        return slab, pl.BlockSpec((slab, cols), lambda *g: (self.step_index(*g), 0))

    def add_cast(self, w):
        _, spec = self._slab_spec(*w.shape)
        self.kinds.append("cast")
        self.in_specs.append(spec)
        self.args.append(w)
        self.out_specs.append(spec)
        self.out_shapes.append(jax.ShapeDtypeStruct(w.shape, BF16))

    def add_norm(self, x, g, mod, sh_idx, sc_idx, rows_per_mod):
        M, D = x.shape
        slab, spec = self._slab_spec(M, D)
        assert rows_per_mod % slab == 0
        per_batch = mod.shape[0] > 1

        def mod_spec(which):
            return pl.BlockSpec((None, 1, D), lambda *g: (
                (self.step_index(*g) * slab) // rows_per_mod if per_batch else 0, 0, which))

        self.kinds.append("norm")
        self.in_specs += [spec, pl.BlockSpec((1, D), lambda *g: (0, 0)), mod_spec(sh_idx), mod_spec(sc_idx)]
        self.args += [x, g.reshape(1, D), mod, mod]
        self.out_specs.append(spec)
        self.out_shapes.append(jax.ShapeDtypeStruct((M, D), BF16))

    @property
    def n_in(self):
        return sum(self.N_INPUTS[k] for k in self.kinds)

    @staticmethod
    def run(kinds, in_refs, out_refs):
        i = 0
        for kind, dst in zip(kinds, out_refs):
            srcs = in_refs[i:i + _Riders.N_INPUTS[kind]]
            i += len(srcs)
            if kind == "cast":
                dst[...] = srcs[0][...].astype(dst.dtype)
            else:
                _prenorm_kernel(*srcs, dst)


def _split_refs(refs, n_in, n_out, rider_kinds):
    n_rin = sum(_Riders.N_INPUTS[k] for k in rider_kinds)
    host_out = refs[n_in + n_rin:n_in + n_rin + n_out]
    run = functools.partial(_Riders.run, rider_kinds, refs[n_in:n_in + n_rin], refs[n_in + n_rin + n_out:])
    return refs[:n_in], host_out, run


def _call_with_riders(kernel_fn, riders, n_in, grid, in_specs, out_specs, out_shapes, args, sem, name):
    kinds = ()
    if riders is not None:
        kinds = tuple(riders.kinds)
        in_specs = in_specs + riders.in_specs
        args = args + riders.args
        out_specs = out_specs + riders.out_specs
        out_shapes = out_shapes + riders.out_shapes
    assert len(in_specs) == n_in + sum(_Riders.N_INPUTS[k] for k in kinds)
    out = pl.pallas_call(
        functools.partial(kernel_fn, rider_kinds=kinds),
        grid=grid,
        in_specs=in_specs,
        out_specs=out_specs,
        out_shape=out_shapes,
        compiler_params=_params(*sem),
        name=name,
    )(*args)
    n_host_out = len(out) - len(kinds)
    return list(out[:n_host_out]), list(out[n_host_out:])


def _lane():
    return lax.broadcasted_iota(jnp.int32, (1, HEAD_DIM), 1)


def _head_norm(y, gain, chunk):
    y2 = y * y
    if chunk == HEAD_DIM:
        ms = jnp.mean(y2, axis=-1, keepdims=True)
    else:
        lo = _lane() < chunk
        s0 = jnp.sum(jnp.where(lo, y2, 0.0), axis=-1, keepdims=True)
        s1 = jnp.sum(jnp.where(lo, 0.0, y2), axis=-1, keepdims=True)
        ms = jnp.where(lo, s0, s1) * (1.0 / chunk)
    return y * lax.rsqrt(ms + EPS) * gain


def _rope(y, cos, sin, chunk):
    rot = chunk // 4
    first = (_lane() % (2 * rot)) < rot
    partner = jnp.where(first, pltpu.roll(y, HEAD_DIM - rot, 1), pltpu.roll(y, rot, 1))
    return y * cos + partner * sin


PROJ_ROW_CHUNK = 128


def _row_chunks(rows):
    step = min(rows, PROJ_ROW_CHUNK)
    return [slice(r, r + step) for r in range(0, rows, step)]


def _qproj_kernel(*refs, chunk, scale, rope):
    if rope:
        a_ref, w_ref, g_ref, cos_ref, sin_ref, o_ref = refs
    else:
        a_ref, w_ref, g_ref, o_ref = refs
    for rows in _row_chunks(a_ref.shape[0]):
        acc = jnp.dot(a_ref[rows, :], w_ref[...], preferred_element_type=F32)
        for h in range(o_ref.shape[0] * GROUP):
            y = _head_norm(acc[:, h * HEAD_DIM:(h + 1) * HEAD_DIM], g_ref[...], chunk)
            if rope:
                y = _rope(y, cos_ref[rows, :], sin_ref[rows, :], chunk)
            o_ref[h // GROUP, h % GROUP, rows, :] = (y * scale).astype(o_ref.dtype)


def _q_proj(h, w_in, col0, n_kv, gain, chunk, tables, seq):
    M, D = h.shape
    tm = _tile(seq, 1024) if tables is not None else _tile(M, 1024)
    kv_per_step = 2 if n_kv % 2 == 0 and col0 % (2 * GROUP * HEAD_DIM) == 0 else 1
    tn = kv_per_step * GROUP * HEAD_DIM
    assert col0 % tn == 0
    rope = tables is not None
    in_specs = [pl.BlockSpec((tm, D), lambda m, n: (m, 0)),
                pl.BlockSpec((D, tn), lambda m, n: (0, col0 // tn + n)),
                pl.BlockSpec((1, HEAD_DIM), lambda m, n: (0, 0))]
    args = [h, w_in, gain]
    if rope:
        per_seq = seq // tm
        in_specs += [pl.BlockSpec((tm, HEAD_DIM), lambda m, n: (m % per_seq, 0))] * 2
        args += list(tables)
    return pl.pallas_call(
        functools.partial(_qproj_kernel, chunk=chunk, scale=chunk ** -0.5 * LOG2E, rope=rope),
        grid=(M // tm, n_kv // kv_per_step),
        in_specs=in_specs,
        out_specs=pl.BlockSpec((kv_per_step, GROUP, tm, HEAD_DIM), lambda m, n: (n, 0, m, 0)),
        out_shape=jax.ShapeDtypeStruct((n_kv, GROUP, M, HEAD_DIM), BF16),
        compiler_params=_params("arbitrary", "arbitrary"),
        name="q_proj",
    )(*args)


def _kvproj_kernel(*refs, chunk, rope, n_kv):
    if rope:
        a_ref, w_ref, g_ref, cos_ref, sin_ref, k_ref, v_ref = refs
    else:
        a_ref, w_ref, g_ref, k_ref, v_ref = refs
    width = n_kv * HEAD_DIM
    for rows in _row_chunks(a_ref.shape[0]):
        acc = jnp.dot(a_ref[rows, :], w_ref[...], preferred_element_type=F32)
        for h in range(n_kv):
            sl = slice(h * HEAD_DIM, (h + 1) * HEAD_DIM)
            y = _head_norm(acc[:, sl], g_ref[...], chunk)
            if rope:
                y = _rope(y, cos_ref[rows, :], sin_ref[rows, :], chunk)
            k_ref[rows, sl] = y.astype(k_ref.dtype)
        v_ref[rows, :] = acc[:, width:].astype(v_ref.dtype)


def _kv_proj(h, w_in, col0, n_kv, gain, chunk, tables, seq, out_dtype):
    M, D = h.shape
    tm = _tile(seq, 1024) if tables is not None else _tile(M, 1024)
    width = n_kv * HEAD_DIM
    tn = 2 * width
    assert col0 % tn == 0
    rope = tables is not None
    in_specs = [pl.BlockSpec((tm, D), lambda m: (m, 0)),
                pl.BlockSpec((D, tn), lambda m: (0, col0 // tn)),
                pl.BlockSpec((1, HEAD_DIM), lambda m: (0, 0))]
    args = [h, w_in, gain]
    if rope:
        per_seq = seq // tm
        in_specs += [pl.BlockSpec((tm, HEAD_DIM), lambda m: (m % per_seq, 0))] * 2
        args += list(tables)
    return pl.pallas_call(
        functools.partial(_kvproj_kernel, chunk=chunk, rope=rope, n_kv=n_kv),
        grid=(M // tm,),
        in_specs=in_specs,
        out_specs=[pl.BlockSpec((tm, width), lambda m: (m, 0))] * 2,
        out_shape=[jax.ShapeDtypeStruct((M, width), out_dtype)] * 2,
        compiler_params=_params("arbitrary"),
        name="kv_proj",
    )(*args)


def _with_ones(v):
    return jnp.concatenate([v, jnp.ones_like(v)], axis=1)


def _weights(s, m):
    return jnp.exp2((s - m).astype(BF16))


def _softmax_sink_pv(scores, values_ones, sink):
    m = sink
    for s in scores:
        m = jnp.maximum(m, jnp.max(s, axis=-1, keepdims=True))
    ox = None
    for s, vx in zip(scores, values_ones):
        pv = jnp.dot(_weights(s, m), vx, preferred_element_type=F32)
        ox = pv if ox is None else ox + pv
    denom = ox[:, HEAD_DIM:] + jnp.exp2(sink - m)
    return ox[:, :HEAD_DIM] / denom


def _qk(q, k):
    return lax.dot_general(q, k, (((1,), (1,)), ((), ())), preferred_element_type=F32)


def _attn_a_ctx_kernel(sink_ref, q_ref, k_ref, v_ref, o_ref, *, n_kv):
    for h in range(n_kv):
        cols = slice(h * HEAD_DIM, (h + 1) * HEAD_DIM)
        k = k_ref[:, cols].astype(BF16)
        vx = _with_ones(v_ref[:, cols].astype(BF16))
        for g in range(GROUP):
            head = h * GROUP + g
            sink = jnp.full((1, 1), sink_ref[head] * LOG2E, F32)
            o = _softmax_sink_pv([_qk(q_ref[h, g], k)], [vx], sink)
            o_ref[:, head * HEAD_DIM:(head + 1) * HEAD_DIM] = o.astype(o_ref.dtype)


def _attn_a_ctx(q, k, v, sink, batch, seq):
    n_kv, _, M, _ = q.shape
    width = n_kv * HEAD_DIM
    return pl.pallas_call(
        functools.partial(_attn_a_ctx_kernel, n_kv=n_kv),
        grid=(batch,),
        in_specs=[pl.BlockSpec(memory_space=pltpu.SMEM),
                  pl.BlockSpec((n_kv, GROUP, seq, HEAD_DIM), lambda b: (0, 0, b, 0)),
                  pl.BlockSpec((seq, width), lambda b: (b, 0)),
                  pl.BlockSpec((seq, width), lambda b: (b, 0))],
        out_specs=pl.BlockSpec((seq, GROUP * width), lambda b: (b, 0)),
        out_shape=jax.ShapeDtypeStruct((M, GROUP * width), BF16),
        compiler_params=_params("arbitrary"),
        name="attn_a_ctx",
    )(sink, q, k, v)


def _attn_a_lat_kernel(*refs, n_blocks, n_kv, rider_kinds):
    (sink_ref, q_ref, k_ref, v_ref, ck_ref, cv_ref), (o_ref,), run_riders = _split_refs(refs, 6, 1, rider_kinds)
    run_riders()
    j = pl.program_id(1)
    rows = GROUP * BLOCK
    span = 3 * BLOCK
    start = pl.multiple_of(jnp.clip(j - 1, 0, n_blocks - 3) * BLOCK, BLOCK)
    row = lax.broadcasted_iota(jnp.int32, (rows, 1), 0)
    qpos = j * BLOCK + row % BLOCK
    kpos = start + lax.broadcasted_iota(jnp.int32, (1, span), 1)
    in_window = jnp.abs(kpos - qpos) <= WINDOW

    for h in range(n_kv):
        cols = slice(h * HEAD_DIM, (h + 1) * HEAD_DIM)
        kw = k_ref[pl.ds(start, span), cols]
        vw = _with_ones(v_ref[pl.ds(start, span), cols])
        kc = ck_ref[:, cols].astype(BF16)
        vc = _with_ones(cv_ref[:, cols].astype(BF16))
        q = q_ref[h].reshape(rows, HEAD_DIM)
        s_w = jnp.where(in_window, _qk(q, kw), NEG)
        s_c = _qk(q, kc)
        sink = jnp.zeros((rows, 1), F32)
        for g in range(GROUP):
            sink = jnp.where(row // BLOCK == g, sink_ref[h * GROUP + g] * LOG2E, sink)
        o = _softmax_sink_pv([s_w, s_c], [vw, vc], sink)
        for g in range(GROUP):
            head = h * GROUP + g
            o_ref[:, head * HEAD_DIM:(head + 1) * HEAD_DIM] = o[g * BLOCK:(g + 1) * BLOCK].astype(o_ref.dtype)


def _attn_a_lat(q, k, v, ck, cv, sink, batch, seq, attach=None):
    n_kv, _, M, _ = q.shape
    n_blocks = seq // BLOCK
    assert n_blocks >= 3
    L = ck.shape[1]
    width = n_kv * HEAD_DIM
    in_specs = [pl.BlockSpec(memory_space=pltpu.SMEM),
                pl.BlockSpec((n_kv, GROUP, BLOCK, HEAD_DIM), lambda b, j: (0, 0, b * n_blocks + j, 0)),
                pl.BlockSpec((seq, width), lambda b, j: (b, 0)),
                pl.BlockSpec((seq, width), lambda b, j: (b, 0)),
                pl.BlockSpec((None, L, width), lambda b, j: (b, 0, 0)),
                pl.BlockSpec((None, L, width), lambda b, j: (b, 0, 0))]
    riders = _Riders(batch * n_blocks, lambda b, j: b * n_blocks + j)
    if attach is not None:
        attach(riders)
    (o,), rider_out = _call_with_riders(
        functools.partial(_attn_a_lat_kernel, n_blocks=n_blocks, n_kv=n_kv), riders, len(in_specs),
        (batch, n_blocks), in_specs,
        [pl.BlockSpec((BLOCK, GROUP * width), lambda b, j: (b * n_blocks + j, 0))],
        [jax.ShapeDtypeStruct((M, GROUP * width), BF16)],
        [sink, q, k, v, ck, cv], ("arbitrary",) * 2, "attn_a_lat")
    return o, rider_out


def _attn_b_kernel(*refs, has_ctx, lam_init, rider_kinds):
    ins, (o_ref,), run_riders = _split_refs(refs, 7 if has_ctx else 5, 1, rider_kinds)
    run_riders()
    if has_ctx:
        lam_ref, subg_ref, q_ref, k_ref, v_ref, ck_ref, cv_ref = ins
    else:
        lam_ref, subg_ref, q_ref, k_ref, v_ref = ins

    lv = lam_ref[...]
    lam = (jnp.exp(jnp.sum(lv[0:1] * lv[1:2], axis=-1, keepdims=True))
           - jnp.exp(jnp.sum(lv[2:3] * lv[3:4], axis=-1, keepdims=True)) + lam_init)

    k = k_ref[...].astype(F32)
    v = v_ref[...].astype(BF16)
    if has_ctx:
        k = jnp.concatenate([k, ck_ref[...]], axis=0)
        v = jnp.concatenate([v, cv_ref[...].astype(BF16)], axis=0)
    vx = _with_ones(v)
    lo = _lane() < DIFF_DIM
    k0 = jnp.where(lo, k, 0.0).astype(BF16)
    k1 = jnp.where(lo, 0.0, k).astype(BF16)

    for g in range(GROUP):
        q = q_ref[g]
        maps = []
        for kc in (k0, k1):
            s = _qk(q, kc)
            ox = jnp.dot(_weights(s, jnp.max(s, axis=-1, keepdims=True)), vx, preferred_element_type=F32)
            maps.append(ox[:, :HEAD_DIM] / ox[:, HEAD_DIM:])
        o = maps[0] - lam * maps[1]
        ms = jnp.mean(o * o, axis=-1, keepdims=True)
        o = o * lax.rsqrt(ms + EPS) * subg_ref[...] * (1.0 - lam_init)
        o_ref[:, g * HEAD_DIM:(g + 1) * HEAD_DIM] = o.astype(o_ref.dtype)


def _attn_b(q, k, v, ctx, lam_vecs, sub_g, lam_init, batch, seq, tq, attach=None):
    n_kv, _, M, _ = q.shape
    n_q = seq // tq
    has_ctx = ctx is not None
    n_steps = batch * n_kv * n_q
    in_specs = [pl.BlockSpec((4, DIFF_DIM), lambda b, h, j: (0, 0)),
                pl.BlockSpec((1, HEAD_DIM), lambda b, h, j: (0, 0)),
                pl.BlockSpec((None, GROUP, tq, HEAD_DIM), lambda b, h, j: (h, 0, b * n_q + j, 0)),
                pl.BlockSpec((seq, HEAD_DIM), lambda b, h, j: (b, h)),
                pl.BlockSpec((seq, HEAD_DIM), lambda b, h, j: (b, h))]
    args = [lam_vecs, sub_g, q, k, v]
    if has_ctx:
        L = ctx[0].shape[1]
        in_specs += [pl.BlockSpec((None, L, HEAD_DIM), lambda b, h, j: (b, 0, h))] * 2
        args += list(ctx)
    out_specs = [pl.BlockSpec((tq, GROUP * HEAD_DIM), lambda b, h, j: (b * n_q + j, h))]
    out_shape = [jax.ShapeDtypeStruct((M, n_kv * GROUP * HEAD_DIM), BF16)]
    riders = _Riders(n_steps, lambda b, h, j: (b * n_kv + h) * n_q + j)
    if attach is not None:
        attach(riders)
    (o,), rider_out = _call_with_riders(
        functools.partial(_attn_b_kernel, has_ctx=has_ctx, lam_init=lam_init), riders, len(in_specs),
        (batch, n_kv, n_q), in_specs, out_specs, out_shape, args, ("arbitrary",) * 3, "attn_b")
    return o, rider_out


def _oproj_kernel(oa_ref, ob_ref, wa_ref, wb_ref, x_ref, gt_ref, y_ref):
    acc = jnp.dot(oa_ref[...], wa_ref[...], preferred_element_type=F32)
    acc = acc + jnp.dot(ob_ref[...], wb_ref[...], preferred_element_type=F32)
    y_ref[...] = x_ref[...] + gt_ref[...] * acc


def _out_proj(o_a, o_b, w_out, x, mod, gt_idx, rows_per_mod):
    M, D = x.shape
    wa = o_a.shape[1]
    wb = o_b.shape[1]
    assert wa == wb and wa + wb == w_out.shape[0]
    tm = _tile(rows_per_mod, 1024) if mod.shape[0] > 1 else _tile(M, 1024)
    tn = _tile(D, 1024)
    return pl.pallas_call(
        _oproj_kernel,
        grid=(M // tm, D // tn),
        in_specs=[pl.BlockSpec((tm, wa), lambda m, n: (m, 0)),
                  pl.BlockSpec((tm, wb), lambda m, n: (m, 0)),
                  pl.BlockSpec((wa, tn), lambda m, n: (0, n)),
                  pl.BlockSpec((wb, tn), lambda m, n: (1, n)),
                  pl.BlockSpec((tm, tn), lambda m, n: (m, n)),
                  _mod_spec(mod, gt_idx, tn, tm, rows_per_mod, 0)],
        out_specs=pl.BlockSpec((tm, tn), lambda m, n: (m, n)),
        out_shape=jax.ShapeDtypeStruct((M, D), F32),
        compiler_params=_params("arbitrary", "arbitrary"),
        name="out_proj",
    )(o_a, o_b, w_out, w_out, x, mod)


def _fc1_kernel(*refs, rider_kinds):
    (a_ref, w_ref), (o_ref,), run_riders = _split_refs(refs, 2, 1, rider_kinds)
    run_riders()
    acc = jnp.dot(a_ref[...], w_ref[...], preferred_element_type=F32)
    r = jnp.maximum(acc, 0.0)
    o_ref[...] = (r * r).astype(o_ref.dtype)


def _fc1(h, w, attach=None):
    M, D = h.shape
    N = w.shape[1]
    tm = _tile(M, 1024)
    tn = _tile(N, 1024)
    n_n = N // tn
    riders = _Riders((M // tm) * n_n, lambda m, n: m * n_n + n)
    if attach is not None:
        attach(riders)
    (hid,), rider_out = _call_with_riders(
        _fc1_kernel, riders, 2, (M // tm, n_n),
        [pl.BlockSpec((tm, D), lambda m, n: (m, 0)), pl.BlockSpec((D, tn), lambda m, n: (0, n))],
        [pl.BlockSpec((tm, tn), lambda m, n: (m, n))],
        [jax.ShapeDtypeStruct((M, N), BF16)],
        [h, w], ("arbitrary",) * 2, "fc1")
    return hid, rider_out


def _fc2_kernel(a_ref, w_ref, x_ref, gt_ref, y_ref):
    k = pl.program_id(2)
    last = pl.num_programs(2) - 1

    def part():
        return jnp.dot(a_ref[...], w_ref[...], preferred_element_type=F32)

    @pl.when(k == 0)
    def _():
        y_ref[...] = part()

    @pl.when(jnp.logical_and(k > 0, k < last))
    def _():
        y_ref[...] += part()

    @pl.when(k == last)
    def _():
        y_ref[...] = x_ref[...] + gt_ref[...] * (y_ref[...] + part())


def _fc2(hid, w, x, mod, gt_idx, rows_per_mod):
    M, D = x.shape
    K = hid.shape[1]
    tm = _tile(rows_per_mod, 1024) if mod.shape[0] > 1 else _tile(M, 1024)
    tn = _tile(D, 1024)
    tk = _tile(K, 4096)
    assert K // tk >= 2
    return pl.pallas_call(
        _fc2_kernel,
        grid=(M // tm, D // tn, K // tk),
        in_specs=[pl.BlockSpec((tm, tk), lambda m, n, k: (m, k)),
                  pl.BlockSpec((tk, tn), lambda m, n, k: (k, n)),
                  pl.BlockSpec((tm, tn), lambda m, n, k: (m, n)),
                  _mod_spec(mod, gt_idx, tn, tm, rows_per_mod, 0)],
        out_specs=pl.BlockSpec((tm, tn), lambda m, n, k: (m, n)),
        out_shape=jax.ShapeDtypeStruct((M, D), F32),
        compiler_params=_params("arbitrary", "arbitrary", "arbitrary"),
        name="fc2",
    )(hid, w, x, mod)


def _rope_tables(seq, chunk):
    n = chunk // 4
    pos = jnp.arange(seq)
    row = (pos // GRID_W).astype(F32)
    col = (pos % GRID_W).astype(F32)
    freq = ROPE_BASE ** (-jnp.arange(n, dtype=F32) / n)
    ra = row[:, None] * freq
    ca = col[:, None] * freq
    cos = jnp.concatenate([jnp.cos(ra)] * 2 + [jnp.cos(ca)] * 2, axis=-1)
    sin = jnp.concatenate([-jnp.sin(ra), jnp.sin(ra), -jnp.sin(ca), jnp.sin(ca)], axis=-1)
    reps = HEAD_DIM // chunk
    return jnp.tile(cos, (1, reps)), jnp.tile(sin, (1, reps))


def _mixer(h, x, mod, w, l, batch, seq, n_kv_a, n_kv_b, ctx_kv, attach_a=None, attach_b=None):
    latent = ctx_kv is not None
    assert latent or attach_a is None
    wa = n_kv_a * GROUP * HEAD_DIM
    ka = n_kv_a * HEAD_DIM
    wb = n_kv_b * GROUP * HEAD_DIM
    tab_a = _rope_tables(seq, HEAD_DIM) if latent else None
    tab_b = _rope_tables(seq, DIFF_DIM) if latent else None
    kv_dtype = BF16 if latent else F32

    gq_a = w["qnorm_a_g"].reshape(1, HEAD_DIM)
    gk_a = w["knorm_a_g"].reshape(1, HEAD_DIM)
    gq_b = jnp.tile(w["qnorm_b_g"], 2).reshape(1, HEAD_DIM)
    gk_b = jnp.tile(w["knorm_b_g"], 2).reshape(1, HEAD_DIM)
    q_a = _q_proj(h, w["w_in"], 0, n_kv_a, gq_a, HEAD_DIM, tab_a, seq)
    k_a, v_a = _kv_proj(h, w["w_in"], wa, n_kv_a, gk_a, HEAD_DIM, tab_a, seq, kv_dtype)
    q_b = _q_proj(h, w["w_in"], wa + 2 * ka, n_kv_b, gq_b, DIFF_DIM, tab_b, seq)
    k_b, v_b = _kv_proj(h, w["w_in"], wa + 2 * ka + wb, n_kv_b, gk_b, DIFF_DIM, tab_b, seq, kv_dtype)

    lam_vecs = jnp.stack([w["lam_q1"], w["lam_k1"], w["lam_q2"], w["lam_k2"]]).astype(F32)
    lam_init = 0.8 - 0.6 * math.exp(-0.3 * l)
    sub_g = w["subln_g"].reshape(1, HEAD_DIM)
    if latent:
        ck_a, cv_a, ck_b, cv_b = ctx_kv
        o_a, ride_a = _attn_a_lat(q_a, k_a, v_a, ck_a, cv_a, w["sink_a"], batch, seq, attach_a)
        ctx_b = (ck_b, cv_b)
    else:
        o_a, ride_a = _attn_a_ctx(q_a, k_a, v_a, w["sink_a"], batch, seq), []
        ctx_b = None
    o_b, ride_b = _attn_b(q_b, k_b, v_b, ctx_b, lam_vecs, sub_g, lam_init, batch, seq, _tile(seq, 256),
                          attach_b)
    x = _out_proj(o_a, o_b, w["w_out"], x, mod, 2, seq)
    return x, (k_a, v_a, k_b, v_b), ride_a, ride_b


def _mlp(h2, x, mod, w_fc1, w_fc2, seq, attach=None):
    hid, ride = _fc1(h2, w_fc1, attach)
    return _fc2(hid, w_fc2, x, mod, 5, seq), ride


def kernel(x_prompt, x_sample, cache_a_k, cache_a_v, cache_b_k, cache_b_v, c, c_ctx, w_mod, b_mod, norm1_g, w_in, qnorm_a_g, knorm_a_g, qnorm_b_g, knorm_b_g, sink_a, lam_q1, lam_k1, lam_q2, lam_k2, subln_g, w_out, norm2_g, w_fc1, w_fc2):
    batch, seq, D = x_prompt.shape
    dbatch, dseq, _ = x_sample.shape
    depth = w_mod.shape[0]
    past = cache_a_k.shape[2]
    n_kv_a = cache_a_k.shape[3]
    n_kv_b = cache_b_k.shape[3]
    assert dseq % GRID_W == 0 and dseq % BLOCK == 0

    per_layer = dict(norm1_g=norm1_g, qnorm_a_g=qnorm_a_g, knorm_a_g=knorm_a_g, qnorm_b_g=qnorm_b_g,
                     knorm_b_g=knorm_b_g, sink_a=sink_a, lam_q1=lam_q1, lam_k1=lam_k1, lam_q2=lam_q2,
                     lam_k2=lam_k2, subln_g=subln_g, norm2_g=norm2_g)
    big = dict(w_in=w_in, w_out=w_out)

    n_rows = -(-(dbatch + 1) // 8) * 8
    cvecs = jnp.zeros((n_rows, D), F32).at[:dbatch].set(c).at[dbatch].set(c_ctx)

    xp = x_prompt.reshape(batch * seq, D)
    xs = x_sample.reshape(dbatch * dseq, D)
    new_kv = []
    for l in range(depth):
        w = {k: v[l] for k, v in per_layer.items()}
        w.update({k: v[l].astype(BF16) for k, v in big.items()})
        mod = _modulation(cvecs, w_mod[l], b_mod[l])
        mod_ctx = mod[dbatch:dbatch + 1].reshape(1, 1, 6 * D)
        mod_lat = mod[:dbatch].reshape(dbatch, 1, 6 * D)

        xs_in = xs
        hp = _prenorm(xp, w["norm1_g"], mod_ctx, 0, 1, seq)
        xp, kv, _, (hs,) = _mixer(
            hp, xp, mod_ctx, w, l, batch, seq, n_kv_a, n_kv_b, None,
            attach_b=lambda r: r.add_norm(xs_in, w["norm1_g"], mod_lat, 0, 1, dseq))
        new_kv.append(kv)
        ctx_kv = (cache_a_k[:, l].reshape(dbatch, past, n_kv_a * HEAD_DIM),
                  cache_a_v[:, l].reshape(dbatch, past, n_kv_a * HEAD_DIM),
                  cache_b_k[:, l].reshape(dbatch, past, n_kv_b * HEAD_DIM),
                  cache_b_v[:, l].reshape(dbatch, past, n_kv_b * HEAD_DIM))
        xp_mid = xp

        def attach_casts(r):
            r.add_cast(w_fc1[l])
            r.add_cast(w_fc2[l])

        xs, _, (hp2,), (fc1_bf16, fc2_bf16) = _mixer(
            hs, xs, mod_lat, w, l, dbatch, dseq, n_kv_a, n_kv_b, ctx_kv,
            attach_a=lambda r: r.add_norm(xp_mid, w["norm2_g"], mod_ctx, 3, 4, seq),
            attach_b=attach_casts)
        xs_mid = xs
        xp, (hs2,) = _mlp(hp2, xp, mod_ctx, fc1_bf16, fc2_bf16, seq,
                          attach=lambda r: r.add_norm(xs_mid, w["norm2_g"], mod_lat, 3, 4, dseq))
        xs, _ = _mlp(hs2, xs, mod_lat, fc1_bf16, fc2_bf16, dseq)

    def stack(i, shape):
        return jnp.stack([kv[i].reshape((batch, seq) + shape) for kv in new_kv], axis=1)

    return (xp.reshape(batch, seq, D), xs.reshape(dbatch, dseq, D),
            stack(0, (n_kv_a, HEAD_DIM)), stack(1, (n_kv_a, HEAD_DIM)),
            stack(2, (n_kv_b, 2, DIFF_DIM)), stack(3, (n_kv_b, HEAD_DIM)))
```

```python
import functools
import math

import jax
import jax.numpy as jnp
from jax import lax
from jax.experimental import pallas as pl
from jax.experimental.pallas import tpu as pltpu

F32 = jnp.float32
BF16 = jnp.bfloat16

HEAD_DIM = 128
BF16_SUBLANES = 16
DIFF_DIM = HEAD_DIM // 2
GROUP = 4
GRID_W = 64
WINDOW = 128
BLOCK = 128
ROPE_BASE = 10000.0
EPS = 1e-6
NEG = -1e30
LOG2E = math.log2(math.e)
V7X_VMEM_LIMIT_BYTES = 56 * 1024 * 1024


def _params(*sem):
    return pltpu.CompilerParams(dimension_semantics=sem, vmem_limit_bytes=V7X_VMEM_LIMIT_BYTES)


def _tile(dim, want):
    t = min(dim, want)
    assert dim % t == 0, (dim, want)
    return t


def _mod_kernel(c_ref, w_ref, b_ref, o_ref):
    c = c_ref[...]
    a = (c * jax.nn.sigmoid(c)).astype(BF16)
    acc = jnp.dot(a, w_ref[...].astype(BF16), preferred_element_type=F32)
    o_ref[...] = acc + b_ref[...]


def _modulation(cvecs, w_mod, b_mod):
    R, D = cvecs.shape
    N = w_mod.shape[1]
    tn = _tile(N, 512)
    return pl.pallas_call(
        _mod_kernel,
        grid=(N // tn,),
        in_specs=[pl.BlockSpec((R, D), lambda n: (0, 0)),
                  pl.BlockSpec((D, tn), lambda n: (0, n)),
                  pl.BlockSpec((1, tn), lambda n: (0, n))],
        out_specs=pl.BlockSpec((R, tn), lambda n: (0, n)),
        out_shape=jax.ShapeDtypeStruct((R, N), F32),
        compiler_params=_params("arbitrary"),
        name="modulation",
    )(cvecs, w_mod, b_mod.reshape(1, N))


def _mod_spec(mod, which, tn, tm, rows_per_mod, n_axis):
    per_batch = mod.shape[0] > 1
    D = mod.shape[2] // 6
    off = which * (D // tn)

    def index(m, *rest):
        n = rest[n_axis] if n_axis is not None else 0
        return ((m * tm) // rows_per_mod if per_batch else 0, 0, off + n)

    return pl.BlockSpec((None, 1, tn), index)


def _prenorm_kernel(x_ref, g_ref, sh_ref, sc_ref, o_ref):
    x = x_ref[...]
    ms = jnp.mean(x * x, axis=-1, keepdims=True)
    y = x * lax.rsqrt(ms + EPS) * g_ref[...]
    o_ref[...] = (y * (1.0 + sc_ref[...]) + sh_ref[...]).astype(o_ref.dtype)


def _prenorm(x, g, mod, sh_idx, sc_idx, rows_per_mod):
    M, D = x.shape
    tm = _tile(M, 256)
    assert rows_per_mod % tm == 0
    return pl.pallas_call(
        _prenorm_kernel,
        grid=(M // tm,),
        in_specs=[pl.BlockSpec((tm, D), lambda m: (m, 0)),
                  pl.BlockSpec((1, D), lambda m: (0, 0)),
                  _mod_spec(mod, sh_idx, D, tm, rows_per_mod, None),
                  _mod_spec(mod, sc_idx, D, tm, rows_per_mod, None)],
        out_specs=pl.BlockSpec((tm, D), lambda m: (m, 0)),
        out_shape=jax.ShapeDtypeStruct((M, D), BF16),
        compiler_params=_params("arbitrary"),
        name="prenorm",
    )(x, g.reshape(1, D), mod, mod)


class _Riders:
    N_INPUTS = {"cast": 1, "norm": 4}

    def __init__(self, n_steps, step_index):
        self.n_steps = n_steps
        self.step_index = step_index
        self.kinds, self.in_specs, self.args, self.out_specs, self.out_shapes = [], [], [], [], []

    def _slab_spec(self, rows, cols):
        slab = rows // self.n_steps
        assert rows % self.n_steps == 0 and slab % BF16_SUBLANES == 0, (rows, self.n_steps)
        return slab, pl.BlockSpec((slab, cols), lambda *g: (self.step_index(*g), 0))

    def add_cast(self, w):
        _, spec = self._slab_spec(*w.shape)
        self.kinds.append("cast")
        self.in_specs.append(spec)
        self.args.append(w)
        self.out_specs.append(spec)
        self.out_shapes.append(jax.ShapeDtypeStruct(w.shape, BF16))

    def add_norm(self, x, g, mod, sh_idx, sc_idx, rows_per_mod):
        M, D = x.shape
        slab, spec = self._slab_spec(M, D)
        assert rows_per_mod % slab == 0
        per_batch = mod.shape[0] > 1

        def mod_spec(which):
            return pl.BlockSpec((None, 1, D), lambda *g: (
                (self.step_index(*g) * slab) // rows_per_mod if per_batch else 0, 0, which))

        self.kinds.append("norm")
        self.in_specs += [spec, pl.BlockSpec((1, D), lambda *g: (0, 0)), mod_spec(sh_idx), mod_spec(sc_idx)]
        self.args += [x, g.reshape(1, D), mod, mod]
        self.out_specs.append(spec)
        self.out_shapes.append(jax.ShapeDtypeStruct((M, D), BF16))

    @property
    def n_in(self):
        return sum(self.N_INPUTS[k] for k in self.kinds)

    @staticmethod
    def run(kinds, in_refs, out_refs):
        i = 0
        for kind, dst in zip(kinds, out_refs):
            srcs = in_refs[i:i + _Riders.N_INPUTS[kind]]
            i += len(srcs)
            if kind == "cast":
                dst[...] = srcs[0][...].astype(dst.dtype)
            else:
                _prenorm_kernel(*srcs, dst)


def _split_refs(refs, n_in, n_out, rider_kinds):
    n_rin = sum(_Riders.N_INPUTS[k] for k in rider_kinds)
    host_out = refs[n_in + n_rin:n_in + n_rin + n_out]
    run = functools.partial(_Riders.run, rider_kinds, refs[n_in:n_in + n_rin], refs[n_in + n_rin + n_out:])
    return refs[:n_in], host_out, run


def _call_with_riders(kernel_fn, riders, n_in, grid, in_specs, out_specs, out_shapes, args, sem, name):
    kinds = ()
    if riders is not None:
        kinds = tuple(riders.kinds)
        in_specs = in_specs + riders.in_specs
        args = args + riders.args
        out_specs = out_specs + riders.out_specs
        out_shapes = out_shapes + riders.out_shapes
    assert len(in_specs) == n_in + sum(_Riders.N_INPUTS[k] for k in kinds)
    out = pl.pallas_call(
        functools.partial(kernel_fn, rider_kinds=kinds),
        grid=grid,
        in_specs=in_specs,
        out_specs=out_specs,
        out_shape=out_shapes,
        compiler_params=_params(*sem),
        name=name,
    )(*args)
    n_host_out = len(out) - len(kinds)
    return list(out[:n_host_out]), list(out[n_host_out:])


def _lane():
    return lax.broadcasted_iota(jnp.int32, (1, HEAD_DIM), 1)


def _head_norm(y, gain, chunk):
    y2 = y * y
    if chunk == HEAD_DIM:
        ms = jnp.mean(y2, axis=-1, keepdims=True)
    else:
        lo = _lane() < chunk
        s0 = jnp.sum(jnp.where(lo, y2, 0.0), axis=-1, keepdims=True)
        s1 = jnp.sum(jnp.where(lo, 0.0, y2), axis=-1, keepdims=True)
        ms = jnp.where(lo, s0, s1) * (1.0 / chunk)
    return y * lax.rsqrt(ms + EPS) * gain


def _rope(y, cos, sin, chunk):
    rot = chunk // 4
    first = (_lane() % (2 * rot)) < rot
    partner = jnp.where(first, pltpu.roll(y, HEAD_DIM - rot, 1), pltpu.roll(y, rot, 1))
    return y * cos + partner * sin


PROJ_ROW_CHUNK = 128


def _row_chunks(rows):
    step = min(rows, PROJ_ROW_CHUNK)
    return [slice(r, r + step) for r in range(0, rows, step)]


def _qproj_kernel(*refs, chunk, scale, rope):
    if rope:
        a_ref, w_ref, g_ref, cos_ref, sin_ref, o_ref = refs
    else:
        a_ref, w_ref, g_ref, o_ref = refs
    for rows in _row_chunks(a_ref.shape[0]):
        acc = jnp.dot(a_ref[rows, :], w_ref[...], preferred_element_type=F32)
        for h in range(o_ref.shape[0] * GROUP):
            y = _head_norm(acc[:, h * HEAD_DIM:(h + 1) * HEAD_DIM], g_ref[...], chunk)
            if rope:
                y = _rope(y, cos_ref[rows, :], sin_ref[rows, :], chunk)
            o_ref[h // GROUP, h % GROUP, rows, :] = (y * scale).astype(o_ref.dtype)


def _q_proj(h, w_in, col0, n_kv, gain, chunk, tables, seq):
    M, D = h.shape
    tm = _tile(seq, 1024) if tables is not None else _tile(M, 1024)
    kv_per_step = 2 if n_kv % 2 == 0 and col0 % (2 * GROUP * HEAD_DIM) == 0 else 1
    tn = kv_per_step * GROUP * HEAD_DIM
    assert col0 % tn == 0
    rope = tables is not None
    in_specs = [pl.BlockSpec((tm, D), lambda m, n: (m, 0)),
                pl.BlockSpec((D, tn), lambda m, n: (0, col0 // tn + n)),
                pl.BlockSpec((1, HEAD_DIM), lambda m, n: (0, 0))]
    args = [h, w_in, gain]
    if rope:
        per_seq = seq // tm
        in_specs += [pl.BlockSpec((tm, HEAD_DIM), lambda m, n: (m % per_seq, 0))] * 2
        args += list(tables)
    return pl.pallas_call(
        functools.partial(_qproj_kernel, chunk=chunk, scale=chunk ** -0.5 * LOG2E, rope=rope),
        grid=(M // tm, n_kv // kv_per_step),
        in_specs=in_specs,
        out_specs=pl.BlockSpec((kv_per_step, GROUP, tm, HEAD_DIM), lambda m, n: (n, 0, m, 0)),
        out_shape=jax.ShapeDtypeStruct((n_kv, GROUP, M, HEAD_DIM), BF16),
        compiler_params=_params("arbitrary", "arbitrary"),
        name="q_proj",
    )(*args)


def _kvproj_kernel(*refs, chunk, rope, n_kv, seq, k_layout, v_layout):
    if rope:
        a_ref, w_ref, g_ref, cos_ref, sin_ref, k_ref, v_ref = refs
    else:
        a_ref, w_ref, g_ref, k_ref, v_ref = refs
    width = n_kv * HEAD_DIM
    for rows in _row_chunks(a_ref.shape[0]):
        acc = jnp.dot(a_ref[rows, :], w_ref[...], preferred_element_type=F32)
        for h in range(n_kv):
            sl = slice(h * HEAD_DIM, (h + 1) * HEAD_DIM)
            y = _head_norm(acc[:, sl], g_ref[...], chunk)
            if rope:
                y = _rope(y, cos_ref[rows, :], sin_ref[rows, :], chunk)
            _store_head(k_ref, k_layout, y, h, n_kv, rows, seq)
            _store_head(v_ref, v_layout, acc[:, width + h * HEAD_DIM:width + (h + 1) * HEAD_DIM], h, n_kv, rows, seq)


def _store_head(ref, layout, y, h, n_kv, rows, seq):
    y = y.astype(ref.dtype) if layout != "transposed" else y
    if layout == "rows":
        ref[rows, h * HEAD_DIM:(h + 1) * HEAD_DIM] = y
    elif layout == "heads":
        ref[pl.ds(rows.start * n_kv + h, rows.stop - rows.start, stride=n_kv), :] = y
    else:
        s, t0 = divmod(rows.start, seq)
        ref[s, h * HEAD_DIM:(h + 1) * HEAD_DIM, t0:t0 + rows.stop - rows.start] = y.T.astype(ref.dtype)


def _load_head(ref, layout, h, n_kv, t0, n):
    if layout == "rows":
        return ref[pl.ds(t0, n), h * HEAD_DIM:(h + 1) * HEAD_DIM]
    return ref[pl.ds(t0 * n_kv + h, n, stride=n_kv), :]


def _head_spec(layout, n_kv, rows, index):
    shape = (rows, n_kv * HEAD_DIM) if layout == "rows" else (rows * n_kv, HEAD_DIM)
    return pl.BlockSpec(shape, lambda *g: (index(*g), 0))


def _kv_proj(h, w_in, col0, n_kv, gain, chunk, tables, seq, out_dtype, k_layout, v_layout):
    M, D = h.shape
    tm = _tile(seq, 1024) if tables is not None else _tile(M, 1024)
    width = n_kv * HEAD_DIM
    tn = 2 * width
    assert col0 % tn == 0 and tm % seq == 0 and seq % PROJ_ROW_CHUNK == 0
    rope = tables is not None

    def out(layout):
        if layout == "transposed":
            return (pl.BlockSpec((tm // seq, width, seq), lambda m: (m, 0, 0)),
                    jax.ShapeDtypeStruct((M // seq, width, seq), out_dtype))
        shape = (M, width) if layout == "rows" else (M * n_kv, HEAD_DIM)
        return _head_spec(layout, n_kv, tm, lambda m: m), jax.ShapeDtypeStruct(shape, out_dtype)

    (k_spec, k_shape), (v_spec, v_shape) = out(k_layout), out(v_layout)
    in_specs = [pl.BlockSpec((tm, D), lambda m: (m, 0)),
                pl.BlockSpec((D, tn), lambda m: (0, col0 // tn)),
                pl.BlockSpec((1, HEAD_DIM), lambda m: (0, 0))]
    args = [h, w_in, gain]
    if rope:
        per_seq = seq // tm
        in_specs += [pl.BlockSpec((tm, HEAD_DIM), lambda m: (m % per_seq, 0))] * 2
        args += list(tables)
    return pl.pallas_call(
        functools.partial(_kvproj_kernel, chunk=chunk, rope=rope, n_kv=n_kv, seq=seq,
                          k_layout=k_layout, v_layout=v_layout),
        grid=(M // tm,),
        in_specs=in_specs,
        out_specs=[k_spec, v_spec],
        out_shape=[k_shape, v_shape],
        compiler_params=_params("arbitrary"),
        name="kv_proj",
    )(*args)


def _with_ones(v):
    return jnp.concatenate([v, jnp.ones_like(v)], axis=1)


def _weights(s, m):
    return jnp.exp2((s - m).astype(BF16))


def _softmax_sink_pv(scores, values_ones, sink):
    m = sink
    for s in scores:
        m = jnp.maximum(m, jnp.max(s, axis=-1, keepdims=True))
    ox = None
    for s, vx in zip(scores, values_ones):
        pv = jnp.dot(_weights(s, m), vx, preferred_element_type=F32)
        ox = pv if ox is None else ox + pv
    denom = ox[:, HEAD_DIM:] + jnp.exp2(sink - m)
    return ox[:, :HEAD_DIM] / denom


def _qk(q, k):
    return lax.dot_general(q, k, (((1,), (1,)), ((), ())), preferred_element_type=F32)


def _attn_a_ctx_kernel(sink_ref, q_ref, k_ref, v_ref, o_ref, *, n_kv, seq):
    for h in range(n_kv):
        k = _load_head(k_ref, "heads", h, n_kv, 0, seq).astype(BF16)
        vx = _with_ones(_load_head(v_ref, "heads", h, n_kv, 0, seq).astype(BF16))
        for g in range(GROUP):
            head = h * GROUP + g
            sink = jnp.full((1, 1), sink_ref[head] * LOG2E, F32)
            o = _softmax_sink_pv([_qk(q_ref[h, g], k)], [vx], sink)
            o_ref[:, head * HEAD_DIM:(head + 1) * HEAD_DIM] = o.astype(o_ref.dtype)


def _attn_a_ctx(q, k, v, sink, batch, seq):
    n_kv, _, M, _ = q.shape
    width = n_kv * HEAD_DIM
    return pl.pallas_call(
        functools.partial(_attn_a_ctx_kernel, n_kv=n_kv, seq=seq),
        grid=(batch,),
        in_specs=[pl.BlockSpec(memory_space=pltpu.SMEM),
                  pl.BlockSpec((n_kv, GROUP, seq, HEAD_DIM), lambda b: (0, 0, b, 0)),
                  _head_spec("heads", n_kv, seq, lambda b: b),
                  _head_spec("heads", n_kv, seq, lambda b: b)],
        out_specs=pl.BlockSpec((seq, GROUP * width), lambda b: (b, 0)),
        out_shape=jax.ShapeDtypeStruct((M, GROUP * width), BF16),
        compiler_params=_params("arbitrary"),
        name="attn_a_ctx",
    )(sink, q, k, v)


ATTN_A_BLOCKS_PER_STEP = 2


def _attn_a_lat_kernel(*refs, n_blocks, n_kv, past, per_step, rider_kinds):
    (sink_ref, q_ref, k_ref, v_ref, ck_ref, cv_ref), (o_ref,), run_riders = _split_refs(refs, 6, 1, rider_kinds)
    run_riders()
    rows = GROUP * BLOCK
    span = 3 * BLOCK
    row = lax.broadcasted_iota(jnp.int32, (rows, 1), 0)
    lane = lax.broadcasted_iota(jnp.int32, (1, span), 1)

    for u in range(per_step):
        j = pl.program_id(1) * per_step + u
        start = pl.multiple_of(jnp.clip(j - 1, 0, n_blocks - 3) * BLOCK, BLOCK)
        in_window = jnp.abs(start + lane - (j * BLOCK + row % BLOCK)) <= WINDOW
        q_rows = slice(u * BLOCK, (u + 1) * BLOCK)
        for h in range(n_kv):
            kw = _load_head(k_ref, "rows", h, n_kv, start, span)
            vw = _with_ones(_load_head(v_ref, "rows", h, n_kv, start, span))
            kc = _load_head(ck_ref, "heads", h, n_kv, 0, past).astype(BF16)
            vc = _with_ones(_load_head(cv_ref, "heads", h, n_kv, 0, past).astype(BF16))
            q = q_ref[h, :, q_rows, :].reshape(rows, HEAD_DIM)
            s_w = jnp.where(in_window, _qk(q, kw), NEG)
            s_c = _qk(q, kc)
            sink = jnp.zeros((rows, 1), F32)
            for g in range(GROUP):
                sink = jnp.where(row // BLOCK == g, sink_ref[h * GROUP + g] * LOG2E, sink)
            o = _softmax_sink_pv([s_w, s_c], [vw, vc], sink)
            for g in range(GROUP):
                head = h * GROUP + g
                o_ref[q_rows, head * HEAD_DIM:(head + 1) * HEAD_DIM] = (
                    o[g * BLOCK:(g + 1) * BLOCK].astype(o_ref.dtype))


def _attn_a_lat(q, k, v, ck, cv, sink, batch, seq, attach=None):
    n_kv, _, M, _ = q.shape
    n_blocks = seq // BLOCK
    per_step = ATTN_A_BLOCKS_PER_STEP if n_blocks % ATTN_A_BLOCKS_PER_STEP == 0 else 1
    n_steps = n_blocks // per_step
    assert n_blocks >= 3
    past = ck.shape[1] // n_kv
    width = n_kv * HEAD_DIM
    tq = per_step * BLOCK
    in_specs = [pl.BlockSpec(memory_space=pltpu.SMEM),
                pl.BlockSpec((n_kv, GROUP, tq, HEAD_DIM), lambda b, j: (0, 0, b * n_steps + j, 0)),
                _head_spec("rows", n_kv, seq, lambda b, j: b),
                _head_spec("rows", n_kv, seq, lambda b, j: b),
                pl.BlockSpec((None, past * n_kv, HEAD_DIM), lambda b, j: (b, 0, 0)),
                pl.BlockSpec((None, past * n_kv, HEAD_DIM), lambda b, j: (b, 0, 0))]
    riders = _Riders(batch * n_steps, lambda b, j: b * n_steps + j)
    if attach is not None:
        attach(riders)
    (o,), rider_out = _call_with_riders(
        functools.partial(_attn_a_lat_kernel, n_blocks=n_blocks, n_kv=n_kv, past=past, per_step=per_step),
        riders, len(in_specs), (batch, n_steps), in_specs,
        [pl.BlockSpec((tq, GROUP * width), lambda b, j: (b * n_steps + j, 0))],
        [jax.ShapeDtypeStruct((M, GROUP * width), BF16)],
        [sink, q, k, v, ck, cv], ("arbitrary",) * 2, "attn_a_lat")
    return o, rider_out


ATTN_B_Q_ROWS = 256
ATTN_B_TILES_PER_STEP = 2


def _attn_b_kernel(*refs, has_ctx, lam_init, n_kv, seq, past, v_layout, rider_kinds):
    ins, (o_ref,), run_riders = _split_refs(refs, 7 if has_ctx else 5, 1, rider_kinds)
    run_riders()
    if has_ctx:
        lam_ref, subg_ref, q_ref, kt_ref, v_ref, ckt_ref, cv_ref = ins
    else:
        lam_ref, subg_ref, q_ref, kt_ref, v_ref = ins
    h = pl.program_id(1)

    lv = lam_ref[...]
    lam = (jnp.exp(jnp.sum(lv[0:1] * lv[1:2], axis=-1, keepdims=True))
           - jnp.exp(jnp.sum(lv[2:3] * lv[3:4], axis=-1, keepdims=True)) + lam_init)

    kt = kt_ref[...].astype(F32)
    if v_layout == "rows":
        v = v_ref[...]
    else:
        v = _load_head(v_ref, "heads", h, n_kv, 0, seq).astype(BF16)
    if has_ctx:
        kt = jnp.concatenate([kt, ckt_ref[...]], axis=1)
        v = jnp.concatenate([v, _load_head(cv_ref, "heads", h, n_kv, 0, past).astype(BF16)], axis=0)
    vx = _with_ones(v)
    lo = lax.broadcasted_iota(jnp.int32, (HEAD_DIM, 1), 0) < DIFF_DIM
    kt0 = jnp.where(lo, kt, 0.0).astype(BF16)
    kt1 = jnp.where(lo, 0.0, kt).astype(BF16)

    tq = q_ref.shape[1]
    for r in range(0, tq, min(tq, ATTN_B_Q_ROWS)):
        q_rows = slice(r, r + min(tq, ATTN_B_Q_ROWS))
        for g in range(GROUP):
            q = q_ref[g, q_rows, :]
            maps = []
            for ktc in (kt0, kt1):
                s = jnp.dot(q, ktc, preferred_element_type=F32)
                ox = jnp.dot(_weights(s, jnp.max(s, axis=-1, keepdims=True)), vx, preferred_element_type=F32)
                maps.append(ox[:, :HEAD_DIM] / ox[:, HEAD_DIM:])
            o = maps[0] - lam * maps[1]
            ms = jnp.mean(o * o, axis=-1, keepdims=True)
            o = o * lax.rsqrt(ms + EPS) * subg_ref[...] * (1.0 - lam_init)
            o_ref[q_rows, g * HEAD_DIM:(g + 1) * HEAD_DIM] = o.astype(o_ref.dtype)


def _attn_b(q, kt, v, v_layout, ctx, lam_vecs, sub_g, lam_init, batch, seq, attach=None):
    n_kv, _, M, _ = q.shape
    tq = _tile(seq, ATTN_B_Q_ROWS * ATTN_B_TILES_PER_STEP)
    n_q = seq // tq
    has_ctx = ctx is not None
    n_steps = batch * n_kv * n_q
    if v_layout == "rows":
        v_spec = pl.BlockSpec((seq, HEAD_DIM), lambda b, h, j: (b, h))
    else:
        v_spec = _head_spec("heads", n_kv, seq, lambda b, h, j: b)
    in_specs = [pl.BlockSpec((4, DIFF_DIM), lambda b, h, j: (0, 0)),
                pl.BlockSpec((1, HEAD_DIM), lambda b, h, j: (0, 0)),
                pl.BlockSpec((None, GROUP, tq, HEAD_DIM), lambda b, h, j: (h, 0, b * n_q + j, 0)),
                pl.BlockSpec((None, HEAD_DIM, seq), lambda b, h, j: (b, h, 0)),
                v_spec]
    args = [lam_vecs, sub_g, q, kt, v]
    past = 0
    if has_ctx:
        past = ctx[0].shape[2]
        in_specs += [pl.BlockSpec((None, HEAD_DIM, past), lambda b, h, j: (b, h, 0)),
                     pl.BlockSpec((None, past * n_kv, HEAD_DIM), lambda b, h, j: (b, 0, 0))]
        args += list(ctx)
    out_specs = [pl.BlockSpec((tq, GROUP * HEAD_DIM), lambda b, h, j: (b * n_q + j, h))]
    out_shape = [jax.ShapeDtypeStruct((M, n_kv * GROUP * HEAD_DIM), BF16)]
    riders = _Riders(n_steps, lambda b, h, j: (b * n_kv + h) * n_q + j)
    if attach is not None:
        attach(riders)
    (o,), rider_out = _call_with_riders(
        functools.partial(_attn_b_kernel, has_ctx=has_ctx, lam_init=lam_init, n_kv=n_kv, seq=seq, past=past,
                          v_layout=v_layout),
        riders, len(in_specs), (batch, n_kv, n_q), in_specs, out_specs, out_shape, args,
        ("arbitrary",) * 3, "attn_b")
    return o, rider_out


def _oproj_kernel(oa_ref, ob_ref, wa_ref, wb_ref, x_ref, gt_ref, y_ref):
    acc = jnp.dot(oa_ref[...], wa_ref[...], preferred_element_type=F32)
    acc = acc + jnp.dot(ob_ref[...], wb_ref[...], preferred_element_type=F32)
    y_ref[...] = x_ref[...] + gt_ref[...] * acc


def _out_proj(o_a, o_b, w_out, x, mod, gt_idx, rows_per_mod):
    M, D = x.shape
    wa = o_a.shape[1]
    wb = o_b.shape[1]
    assert wa == wb and wa + wb == w_out.shape[0]
    tm = _tile(rows_per_mod, 1024) if mod.shape[0] > 1 else _tile(M, 1024)
    tn = _tile(D, 1024)
    return pl.pallas_call(
        _oproj_kernel,
        grid=(M // tm, D // tn),
        in_specs=[pl.BlockSpec((tm, wa), lambda m, n: (m, 0)),
                  pl.BlockSpec((tm, wb), lambda m, n: (m, 0)),
                  pl.BlockSpec((wa, tn), lambda m, n: (0, n)),
                  pl.BlockSpec((wb, tn), lambda m, n: (1, n)),
                  pl.BlockSpec((tm, tn), lambda m, n: (m, n)),
                  _mod_spec(mod, gt_idx, tn, tm, rows_per_mod, 0)],
        out_specs=pl.BlockSpec((tm, tn), lambda m, n: (m, n)),
        out_shape=jax.ShapeDtypeStruct((M, D), F32),
        compiler_params=_params("arbitrary", "arbitrary"),
        name="out_proj",
    )(o_a, o_b, w_out, w_out, x, mod)


def _fc1_kernel(*refs, rider_kinds):
    (a_ref, w_ref), (o_ref,), run_riders = _split_refs(refs, 2, 1, rider_kinds)
    run_riders()
    acc = jnp.dot(a_ref[...], w_ref[...], preferred_element_type=F32)
    r = jnp.maximum(acc, 0.0)
    o_ref[...] = (r * r).astype(o_ref.dtype)


def _fc1(h, w, attach=None):
    M, D = h.shape
    N = w.shape[1]
    tm = _tile(M, 1024)
    tn = _tile(N, 1024)
    n_n = N // tn
    riders = _Riders((M // tm) * n_n, lambda m, n: m * n_n + n)
    if attach is not None:
        attach(riders)
    (hid,), rider_out = _call_with_riders(
        _fc1_kernel, riders, 2, (M // tm, n_n),
        [pl.BlockSpec((tm, D), lambda m, n: (m, 0)), pl.BlockSpec((D, tn), lambda m, n: (0, n))],
        [pl.BlockSpec((tm, tn), lambda m, n: (m, n))],
        [jax.ShapeDtypeStruct((M, N), BF16)],
        [h, w], ("arbitrary",) * 2, "fc1")
    return hid, rider_out


def _fc2_kernel(a_ref, w_ref, x_ref, gt_ref, y_ref):
    k = pl.program_id(2)
    last = pl.num_programs(2) - 1

    def part():
        return jnp.dot(a_ref[...], w_ref[...], preferred_element_type=F32)

    @pl.when(k == 0)
    def _():
        y_ref[...] = part()

    @pl.when(jnp.logical_and(k > 0, k < last))
    def _():
        y_ref[...] += part()

    @pl.when(k == last)
    def _():
        y_ref[...] = x_ref[...] + gt_ref[...] * (y_ref[...] + part())


def _fc2(hid, w, x, mod, gt_idx, rows_per_mod):
    M, D = x.shape
    K = hid.shape[1]
    tm = _tile(rows_per_mod, 1024) if mod.shape[0] > 1 else _tile(M, 1024)
    tn = _tile(D, 1024)
    tk = _tile(K, 4096)
    assert K // tk >= 2
    return pl.pallas_call(
        _fc2_kernel,
        grid=(M // tm, D // tn, K // tk),
        in_specs=[pl.BlockSpec((tm, tk), lambda m, n, k: (m, k)),
                  pl.BlockSpec((tk, tn), lambda m, n, k: (k, n)),
                  pl.BlockSpec((tm, tn), lambda m, n, k: (m, n)),
                  _mod_spec(mod, gt_idx, tn, tm, rows_per_mod, 0)],
        out_specs=pl.BlockSpec((tm, tn), lambda m, n, k: (m, n)),
        out_shape=jax.ShapeDtypeStruct((M, D), F32),
        compiler_params=_params("arbitrary", "arbitrary", "arbitrary"),
        name="fc2",
    )(hid, w, x, mod)


def _rope_tables(seq, chunk):
    n = chunk // 4
    pos = jnp.arange(seq)
    row = (pos // GRID_W).astype(F32)
    col = (pos % GRID_W).astype(F32)
    freq = ROPE_BASE ** (-jnp.arange(n, dtype=F32) / n)
    ra = row[:, None] * freq
    ca = col[:, None] * freq
    cos = jnp.concatenate([jnp.cos(ra)] * 2 + [jnp.cos(ca)] * 2, axis=-1)
    sin = jnp.concatenate([-jnp.sin(ra), jnp.sin(ra), -jnp.sin(ca), jnp.sin(ca)], axis=-1)
    reps = HEAD_DIM // chunk
    return jnp.tile(cos, (1, reps)), jnp.tile(sin, (1, reps))


def _mixer(h, x, mod, w, l, batch, seq, n_kv_a, n_kv_b, ctx_kv, attach_a=None, attach_b=None):
    latent = ctx_kv is not None
    assert latent or attach_a is None
    wa = n_kv_a * GROUP * HEAD_DIM
    ka = n_kv_a * HEAD_DIM
    wb = n_kv_b * GROUP * HEAD_DIM
    tab_a = _rope_tables(seq, HEAD_DIM) if latent else None
    tab_b = _rope_tables(seq, DIFF_DIM) if latent else None
    kv_dtype = BF16 if latent else F32
    kv_layout = "rows" if latent else "heads"

    gq_a = w["qnorm_a_g"].reshape(1, HEAD_DIM)
    gk_a = w["knorm_a_g"].reshape(1, HEAD_DIM)
    gq_b = jnp.tile(w["qnorm_b_g"], 2).reshape(1, HEAD_DIM)
    gk_b = jnp.tile(w["knorm_b_g"], 2).reshape(1, HEAD_DIM)
    q_a = _q_proj(h, w["w_in"], 0, n_kv_a, gq_a, HEAD_DIM, tab_a, seq)
    k_a, v_a = _kv_proj(h, w["w_in"], wa, n_kv_a, gk_a, HEAD_DIM, tab_a, seq, kv_dtype, kv_layout, kv_layout)
    q_b = _q_proj(h, w["w_in"], wa + 2 * ka, n_kv_b, gq_b, DIFF_DIM, tab_b, seq)
    kt_b, v_b = _kv_proj(h, w["w_in"], wa + 2 * ka + wb, n_kv_b, gk_b, DIFF_DIM, tab_b, seq, kv_dtype,
                         "transposed", kv_layout)

    lam_vecs = jnp.stack([w["lam_q1"], w["lam_k1"], w["lam_q2"], w["lam_k2"]]).astype(F32)
    lam_init = 0.8 - 0.6 * math.exp(-0.3 * l)
    sub_g = w["subln_g"].reshape(1, HEAD_DIM)
    if latent:
        ck_a, cv_a, ckt_b, cv_b = ctx_kv
        o_a, ride_a = _attn_a_lat(q_a, k_a, v_a, ck_a, cv_a, w["sink_a"], batch, seq, attach_a)
        ctx_b = (ckt_b, cv_b)
    else:
        o_a, ride_a = _attn_a_ctx(q_a, k_a, v_a, w["sink_a"], batch, seq), []
        ctx_b = None
    o_b, ride_b = _attn_b(q_b, kt_b, v_b, kv_layout, ctx_b, lam_vecs, sub_g, lam_init, batch, seq, attach_b)
    x = _out_proj(o_a, o_b, w["w_out"], x, mod, 2, seq)
    return x, (k_a, v_a, kt_b, v_b), ride_a, ride_b


def _mlp(h2, x, mod, w_fc1, w_fc2, seq, attach=None):
    hid, ride = _fc1(h2, w_fc1, attach)
    return _fc2(hid, w_fc2, x, mod, 5, seq), ride


def kernel(x_prompt, x_sample, cache_a_k, cache_a_v, cache_b_k, cache_b_v, c, c_ctx, w_mod, b_mod, norm1_g, w_in, qnorm_a_g, knorm_a_g, qnorm_b_g, knorm_b_g, sink_a, lam_q1, lam_k1, lam_q2, lam_k2, subln_g, w_out, norm2_g, w_fc1, w_fc2):
    batch, seq, D = x_prompt.shape
    dbatch, dseq, _ = x_sample.shape
    depth = w_mod.shape[0]
    past = cache_a_k.shape[2]
    n_kv_a = cache_a_k.shape[3]
    n_kv_b = cache_b_k.shape[3]
    assert dseq % GRID_W == 0 and dseq % BLOCK == 0

    per_layer = dict(norm1_g=norm1_g, qnorm_a_g=qnorm_a_g, knorm_a_g=knorm_a_g, qnorm_b_g=qnorm_b_g,
                     knorm_b_g=knorm_b_g, sink_a=sink_a, lam_q1=lam_q1, lam_k1=lam_k1, lam_q2=lam_q2,
                     lam_k2=lam_k2, subln_g=subln_g, norm2_g=norm2_g)
    big = dict(w_in=w_in, w_out=w_out)

    n_rows = -(-(dbatch + 1) // 8) * 8
    cvecs = jnp.zeros((n_rows, D), F32).at[:dbatch].set(c).at[dbatch].set(c_ctx)

    xp = x_prompt.reshape(batch * seq, D)
    xs = x_sample.reshape(dbatch * dseq, D)
    new_kv = []
    for l in range(depth):
        w = {k: v[l] for k, v in per_layer.items()}
        w.update({k: v[l].astype(BF16) for k, v in big.items()})
        mod = _modulation(cvecs, w_mod[l], b_mod[l])
        mod_ctx = mod[dbatch:dbatch + 1].reshape(1, 1, 6 * D)
        mod_lat = mod[:dbatch].reshape(dbatch, 1, 6 * D)

        xs_in = xs
        hp = _prenorm(xp, w["norm1_g"], mod_ctx, 0, 1, seq)
        xp, kv, _, (hs,) = _mixer(
            hp, xp, mod_ctx, w, l, batch, seq, n_kv_a, n_kv_b, None,
            attach_b=lambda r: r.add_norm(xs_in, w["norm1_g"], mod_lat, 0, 1, dseq))
        new_kv.append(kv)
        ctx_kv = (cache_a_k[:, l].reshape(dbatch, past * n_kv_a, HEAD_DIM),
                  cache_a_v[:, l].reshape(dbatch, past * n_kv_a, HEAD_DIM),
                  jnp.transpose(cache_b_k[:, l], (0, 2, 3, 4, 1)).reshape(dbatch, n_kv_b * HEAD_DIM, past),
                  cache_b_v[:, l].reshape(dbatch, past * n_kv_b, HEAD_DIM))
        xp_mid = xp

        def attach_casts(r):
            r.add_cast(w_fc1[l])
            r.add_cast(w_fc2[l])

        xs, _, (hp2,), (fc1_bf16, fc2_bf16) = _mixer(
            hs, xs, mod_lat, w, l, dbatch, dseq, n_kv_a, n_kv_b, ctx_kv,
            attach_a=lambda r: r.add_norm(xp_mid, w["norm2_g"], mod_ctx, 3, 4, seq),
            attach_b=attach_casts)
        xs_mid = xs
        xp, (hs2,) = _mlp(hp2, xp, mod_ctx, fc1_bf16, fc2_bf16, seq,
                          attach=lambda r: r.add_norm(xs_mid, w["norm2_g"], mod_lat, 3, 4, dseq))
        xs, _ = _mlp(hs2, xs, mod_lat, fc1_bf16, fc2_bf16, dseq)

    def stack(i, to_cache_shape):
        return jnp.stack([to_cache_shape(kv[i]) for kv in new_kv], axis=1)

    def heads_a(t):
        return t.reshape(batch, seq, n_kv_a, HEAD_DIM)

    def heads_b(t):
        return t.reshape(batch, seq, n_kv_b, HEAD_DIM)

    def transposed_b(t):
        return jnp.transpose(t.reshape(batch, n_kv_b, 2, DIFF_DIM, seq), (0, 4, 1, 2, 3))

    return (xp.reshape(batch, seq, D), xs.reshape(dbatch, dseq, D),
            stack(0, heads_a), stack(1, heads_a), stack(2, transposed_b), stack(3, heads_b))
```

```python
import functools
import math

import jax
import jax.numpy as jnp
from jax import lax
from jax.experimental import pallas as pl
from jax.experimental.pallas import tpu as pltpu

F32 = jnp.float32
BF16 = jnp.bfloat16

HEAD_DIM = 128
BF16_SUBLANES = 16
DIFF_DIM = HEAD_DIM // 2
GROUP = 4
GRID_W = 64
WINDOW = 128
BLOCK = 128
ROPE_BASE = 10000.0
EPS = 1e-6
NEG = -1e30
LOG2E = math.log2(math.e)
V7X_VMEM_LIMIT_BYTES = 56 * 1024 * 1024


def _params(*sem):
    return pltpu.CompilerParams(dimension_semantics=sem, vmem_limit_bytes=V7X_VMEM_LIMIT_BYTES)


def _tile(dim, want):
    t = min(dim, want)
    assert dim % t == 0, (dim, want)
    return t


def _mod_kernel(c_ref, w_ref, b_ref, o_ref):
    c = c_ref[...]
    a = (c * jax.nn.sigmoid(c)).astype(BF16)
    acc = jnp.dot(a, w_ref[...].astype(BF16), preferred_element_type=F32)
    o_ref[...] = acc + b_ref[...]


def _modulation(cvecs, w_mod, b_mod):
    R, D = cvecs.shape
    N = w_mod.shape[1]
    tn = _tile(N, 512)
    return pl.pallas_call(
        _mod_kernel,
        grid=(N // tn,),
        in_specs=[pl.BlockSpec((R, D), lambda n: (0, 0)),
                  pl.BlockSpec((D, tn), lambda n: (0, n)),
                  pl.BlockSpec((1, tn), lambda n: (0, n))],
        out_specs=pl.BlockSpec((R, tn), lambda n: (0, n)),
        out_shape=jax.ShapeDtypeStruct((R, N), F32),
        compiler_params=_params("arbitrary"),
        name="modulation",
    )(cvecs, w_mod, b_mod.reshape(1, N))


def _mod_spec(mod, which, tn, tm, rows_per_mod, n_axis):
    per_batch = mod.shape[0] > 1
    D = mod.shape[2] // 6
    off = which * (D // tn)

    def index(m, *rest):
        n = rest[n_axis] if n_axis is not None else 0
        return ((m * tm) // rows_per_mod if per_batch else 0, 0, off + n)

    return pl.BlockSpec((None, 1, tn), index)


def _prenorm_kernel(x_ref, g_ref, sh_ref, sc_ref, o_ref):
    x = x_ref[...]
    ms = jnp.mean(x * x, axis=-1, keepdims=True)
    y = x * lax.rsqrt(ms + EPS) * g_ref[...]
    o_ref[...] = (y * (1.0 + sc_ref[...]) + sh_ref[...]).astype(o_ref.dtype)


def _prenorm(x, g, mod, sh_idx, sc_idx, rows_per_mod):
    M, D = x.shape
    tm = _tile(M, 256)
    assert rows_per_mod % tm == 0
    return pl.pallas_call(
        _prenorm_kernel,
        grid=(M // tm,),
        in_specs=[pl.BlockSpec((tm, D), lambda m: (m, 0)),
                  pl.BlockSpec((1, D), lambda m: (0, 0)),
                  _mod_spec(mod, sh_idx, D, tm, rows_per_mod, None),
                  _mod_spec(mod, sc_idx, D, tm, rows_per_mod, None)],
        out_specs=pl.BlockSpec((tm, D), lambda m: (m, 0)),
        out_shape=jax.ShapeDtypeStruct((M, D), BF16),
        compiler_params=_params("arbitrary"),
        name="prenorm",
    )(x, g.reshape(1, D), mod, mod)


class _Riders:
    N_INPUTS = {"cast": 1, "norm": 4}

    def __init__(self, n_steps, step_index):
        self.n_steps = n_steps
        self.step_index = step_index
        self.kinds, self.in_specs, self.args, self.out_specs, self.out_shapes = [], [], [], [], []

    def _slab_spec(self, rows, cols):
        slab = rows // self.n_steps
        assert rows % self.n_steps == 0 and slab % BF16_SUBLANES == 0, (rows, self.n_steps)
        return slab, pl.BlockSpec((slab, cols), lambda *g: (self.step_index(*g), 0))

    def add_cast(self, w):
        _, spec = self._slab_spec(*w.shape)
        self.kinds.append("cast")
        self.in_specs.append(spec)
        self.args.append(w)
        self.out_specs.append(spec)
        self.out_shapes.append(jax.ShapeDtypeStruct(w.shape, BF16))

    def add_norm(self, x, g, mod, sh_idx, sc_idx, rows_per_mod):
        M, D = x.shape
        slab, spec = self._slab_spec(M, D)
        assert rows_per_mod % slab == 0
        per_batch = mod.shape[0] > 1

        def mod_spec(which):
            return pl.BlockSpec((None, 1, D), lambda *g: (
                (self.step_index(*g) * slab) // rows_per_mod if per_batch else 0, 0, which))

        self.kinds.append("norm")
        self.in_specs += [spec, pl.BlockSpec((1, D), lambda *g: (0, 0)), mod_spec(sh_idx), mod_spec(sc_idx)]
        self.args += [x, g.reshape(1, D), mod, mod]
        self.out_specs.append(spec)
        self.out_shapes.append(jax.ShapeDtypeStruct((M, D), BF16))

    @property
    def n_in(self):
        return sum(self.N_INPUTS[k] for k in self.kinds)

    @staticmethod
    def run(kinds, in_refs, out_refs):
        i = 0
        for kind, dst in zip(kinds, out_refs):
            srcs = in_refs[i:i + _Riders.N_INPUTS[kind]]
            i += len(srcs)
            if kind == "cast":
                dst[...] = srcs[0][...].astype(dst.dtype)
            else:
                _prenorm_kernel(*srcs, dst)


def _split_refs(refs, n_in, n_out, rider_kinds):
    n_rin = sum(_Riders.N_INPUTS[k] for k in rider_kinds)
    host_out = refs[n_in + n_rin:n_in + n_rin + n_out]
    run = functools.partial(_Riders.run, rider_kinds, refs[n_in:n_in + n_rin], refs[n_in + n_rin + n_out:])
    return refs[:n_in], host_out, run


def _call_with_riders(kernel_fn, riders, n_in, grid, in_specs, out_specs, out_shapes, args, sem, name):
    kinds = ()
    if riders is not None:
        kinds = tuple(riders.kinds)
        in_specs = in_specs + riders.in_specs
        args = args + riders.args
        out_specs = out_specs + riders.out_specs
        out_shapes = out_shapes + riders.out_shapes
    assert len(in_specs) == n_in + sum(_Riders.N_INPUTS[k] for k in kinds)
    out = pl.pallas_call(
        functools.partial(kernel_fn, rider_kinds=kinds),
        grid=grid,
        in_specs=in_specs,
        out_specs=out_specs,
        out_shape=out_shapes,
        compiler_params=_params(*sem),
        name=name,
    )(*args)
    n_host_out = len(out) - len(kinds)
    return list(out[:n_host_out]), list(out[n_host_out:])


def _lane():
    return lax.broadcasted_iota(jnp.int32, (1, HEAD_DIM), 1)


def _head_norm(y, gain, chunk):
    y2 = y * y
    if chunk == HEAD_DIM:
        ms = jnp.mean(y2, axis=-1, keepdims=True)
    else:
        lo = _lane() < chunk
        s0 = jnp.sum(jnp.where(lo, y2, 0.0), axis=-1, keepdims=True)
        s1 = jnp.sum(jnp.where(lo, 0.0, y2), axis=-1, keepdims=True)
        ms = jnp.where(lo, s0, s1) * (1.0 / chunk)
    return y * lax.rsqrt(ms + EPS) * gain


def _rope(y, cos, sin, chunk):
    rot = chunk // 4
    first = (_lane() % (2 * rot)) < rot
    partner = jnp.where(first, pltpu.roll(y, HEAD_DIM - rot, 1), pltpu.roll(y, rot, 1))
    return y * cos + partner * sin


PROJ_ROW_CHUNK = 128


def _row_chunks(rows):
    step = min(rows, PROJ_ROW_CHUNK)
    return [slice(r, r + step) for r in range(0, rows, step)]


def _q_tile(a_ref, w_ref, g_ref, tab, o_ref, chunk):
    scale = chunk ** -0.5 * LOG2E
    for rows in _row_chunks(a_ref.shape[0]):
        acc = jnp.dot(a_ref[rows, :], w_ref[...], preferred_element_type=F32)
        for h in range(o_ref.shape[0] * GROUP):
            y = _head_norm(acc[:, h * HEAD_DIM:(h + 1) * HEAD_DIM], g_ref[...], chunk)
            if tab is not None:
                y = _rope(y, tab[0][rows, :], tab[1][rows, :], chunk)
            o_ref[h // GROUP, h % GROUP, rows, :] = (y * scale).astype(o_ref.dtype)


def _kv_tile(a_ref, w_ref, g_ref, tab, k_ref, v_ref, chunk, n_kv, seq, k_layout, v_layout):
    width = n_kv * HEAD_DIM
    for rows in _row_chunks(a_ref.shape[0]):
        acc = jnp.dot(a_ref[rows, :], w_ref[...], preferred_element_type=F32)
        for h in range(n_kv):
            y = _head_norm(acc[:, h * HEAD_DIM:(h + 1) * HEAD_DIM], g_ref[...], chunk)
            if tab is not None:
                y = _rope(y, tab[0][rows, :], tab[1][rows, :], chunk)
            _store_head(k_ref, k_layout, y, h, n_kv, rows, seq)
            _store_head(v_ref, v_layout, acc[:, width + h * HEAD_DIM:width + (h + 1) * HEAD_DIM], h, n_kv, rows, seq)


def _store_head(ref, layout, y, h, n_kv, rows, seq):
    y = y.astype(ref.dtype) if layout != "transposed" else y
    if layout == "rows":
        ref[rows, h * HEAD_DIM:(h + 1) * HEAD_DIM] = y
    elif layout == "heads":
        ref[pl.ds(rows.start * n_kv + h, rows.stop - rows.start, stride=n_kv), :] = y
    else:
        s, t0 = divmod(rows.start, seq)
        ref[s, h * HEAD_DIM:(h + 1) * HEAD_DIM, t0:t0 + rows.stop - rows.start] = y.T.astype(ref.dtype)


def _load_head(ref, layout, h, n_kv, t0, n):
    if layout == "rows":
        return ref[pl.ds(t0, n), h * HEAD_DIM:(h + 1) * HEAD_DIM]
    return ref[pl.ds(t0 * n_kv + h, n, stride=n_kv), :]


def _head_spec(layout, n_kv, rows, index):
    shape = (rows, n_kv * HEAD_DIM) if layout == "rows" else (rows * n_kv, HEAD_DIM)
    return pl.BlockSpec(shape, lambda *g: (index(*g), 0))


_QA, _KVA, _QB, _KVB = (0, 2), (2, 3), (3, 5), (5, 6)


def _inproj_kernel(*refs, rope, n_kv, seq, kv_layout):
    a_ref, w_ref, gqa_ref, gka_ref, gqb_ref, gkb_ref = refs[:6]
    tab_a = tab_b = None
    if rope:
        tab_a, tab_b = refs[6:8], refs[8:10]
    qa_ref, ka_ref, va_ref, qb_ref, ktb_ref, vb_ref = refs[-6:]
    n = pl.program_id(1)

    def on(tiles):
        return jnp.logical_and(n >= tiles[0], n < tiles[1])

    @pl.when(on(_QA))
    def _():
        _q_tile(a_ref, w_ref, gqa_ref, tab_a, qa_ref, HEAD_DIM)

    @pl.when(on(_KVA))
    def _():
        _kv_tile(a_ref, w_ref, gka_ref, tab_a, ka_ref, va_ref, HEAD_DIM, n_kv, seq, kv_layout, kv_layout)

    @pl.when(on(_QB))
    def _():
        _q_tile(a_ref, w_ref, gqb_ref, tab_b, qb_ref, DIFF_DIM)

    @pl.when(on(_KVB))
    def _():
        _kv_tile(a_ref, w_ref, gkb_ref, tab_b, ktb_ref, vb_ref, DIFF_DIM, n_kv, seq, "transposed", kv_layout)


def _in_proj(h, w_in, gains, tables, seq, n_kv, kv_dtype, kv_layout, tm):
    M, D = h.shape
    width = n_kv * HEAD_DIM
    tn = 2 * width
    assert w_in.shape[1] == 6 * tn and M % tm == 0 and (tm % seq == 0 or seq % tm == 0)
    assert seq % PROJ_ROW_CHUNK == 0
    kv_per_tile = tn // (GROUP * HEAD_DIM)
    rope = tables is not None
    in_specs = [pl.BlockSpec((tm, D), lambda m, n: (m, 0)),
                pl.BlockSpec((D, tn), lambda m, n: (0, n))]
    in_specs += [pl.BlockSpec((1, HEAD_DIM), lambda m, n: (0, 0))] * 4
    args = [h, w_in] + list(gains)
    if rope:
        per_seq = seq // tm
        in_specs += [pl.BlockSpec((tm, HEAD_DIM), lambda m, n: (m % per_seq, 0))] * 4
        args += list(tables[0]) + list(tables[1])

    def q_out(first):
        return (pl.BlockSpec((kv_per_tile, GROUP, tm, HEAD_DIM),
                             lambda m, n: (jnp.clip(n - first, 0, 1), 0, m, 0)),
                jax.ShapeDtypeStruct((n_kv, GROUP, M, HEAD_DIM), BF16))

    def kv_out(layout):
        if layout == "transposed":
            assert tm % seq == 0
            return (pl.BlockSpec((tm // seq, width, seq), lambda m, n: (m, 0, 0)),
                    jax.ShapeDtypeStruct((M // seq, width, seq), kv_dtype))
        shape = (M, width) if layout == "rows" else (M * n_kv, HEAD_DIM)
        return _head_spec(layout, n_kv, tm, lambda m, n: m), jax.ShapeDtypeStruct(shape, kv_dtype)

    outs = [q_out(_QA[0]), kv_out(kv_layout), kv_out(kv_layout),
            q_out(_QB[0]), kv_out("transposed"), kv_out(kv_layout)]
    return pl.pallas_call(
        functools.partial(_inproj_kernel, rope=rope, n_kv=n_kv, seq=seq, kv_layout=kv_layout),
        grid=(M // tm, 6),
        in_specs=in_specs,
        out_specs=[o[0] for o in outs],
        out_shape=[o[1] for o in outs],
        compiler_params=_params("arbitrary", "arbitrary"),
        name="in_proj",
    )(*args)


def _with_ones(v):
    return jnp.concatenate([v, jnp.ones_like(v)], axis=1)


def _weights(s, m):
    return jnp.exp2((s - m).astype(BF16))


def _softmax_sink_pv(scores, values_ones, sink):
    m = sink
    for s in scores:
        m = jnp.maximum(m, jnp.max(s, axis=-1, keepdims=True))
    ox = None
    for s, vx in zip(scores, values_ones):
        pv = jnp.dot(_weights(s, m), vx, preferred_element_type=F32)
        ox = pv if ox is None else ox + pv
    denom = ox[:, HEAD_DIM:] + jnp.exp2(sink - m)
    return ox[:, :HEAD_DIM] / denom


def _qk(q, k):
    return lax.dot_general(q, k, (((1,), (1,)), ((), ())), preferred_element_type=F32)


def _attn_a_ctx_kernel(*refs, n_kv, seq, rider_kinds):
    (sink_ref, q_ref, k_ref, v_ref), (o_ref,), run_riders = _split_refs(refs, 4, 1, rider_kinds)
    run_riders()
    for h in range(n_kv):
        k = _load_head(k_ref, "heads", h, n_kv, 0, seq).astype(BF16)
        vx = _with_ones(_load_head(v_ref, "heads", h, n_kv, 0, seq).astype(BF16))
        for g in range(GROUP):
            head = h * GROUP + g
            sink = jnp.full((1, 1), sink_ref[head] * LOG2E, F32)
            o = _softmax_sink_pv([_qk(q_ref[h, g], k)], [vx], sink)
            o_ref[:, head * HEAD_DIM:(head + 1) * HEAD_DIM] = o.astype(o_ref.dtype)


def _attn_a_ctx(q, k, v, sink, batch, seq, attach=None):
    n_kv, _, M, _ = q.shape
    width = n_kv * HEAD_DIM
    in_specs = [pl.BlockSpec(memory_space=pltpu.SMEM),
                pl.BlockSpec((n_kv, GROUP, seq, HEAD_DIM), lambda b: (0, 0, b, 0)),
                _head_spec("heads", n_kv, seq, lambda b: b),
                _head_spec("heads", n_kv, seq, lambda b: b)]
    riders = _Riders(batch, lambda b: b)
    if attach is not None:
        attach(riders)
    (o,), rider_out = _call_with_riders(
        functools.partial(_attn_a_ctx_kernel, n_kv=n_kv, seq=seq), riders, len(in_specs), (batch,), in_specs,
        [pl.BlockSpec((seq, GROUP * width), lambda b: (b, 0))],
        [jax.ShapeDtypeStruct((M, GROUP * width), BF16)],
        [sink, q, k, v], ("arbitrary",), "attn_a_ctx")
    return o, rider_out


ATTN_A_BLOCKS_PER_STEP = 2


def _attn_a_lat_kernel(*refs, n_blocks, n_kv, past, per_step, rider_kinds):
    (sink_ref, q_ref, k_ref, v_ref, ck_ref, cv_ref), (o_ref,), run_riders = _split_refs(refs, 6, 1, rider_kinds)
    run_riders()
    rows = GROUP * BLOCK
    span = 3 * BLOCK
    row = lax.broadcasted_iota(jnp.int32, (rows, 1), 0)
    lane = lax.broadcasted_iota(jnp.int32, (1, span), 1)

    for u in range(per_step):
        j = pl.program_id(1) * per_step + u
        start = pl.multiple_of(jnp.clip(j - 1, 0, n_blocks - 3) * BLOCK, BLOCK)
        in_window = jnp.abs(start + lane - (j * BLOCK + row % BLOCK)) <= WINDOW
        q_rows = slice(u * BLOCK, (u + 1) * BLOCK)
        for h in range(n_kv):
            kw = _load_head(k_ref, "rows", h, n_kv, start, span)
            vw = _with_ones(_load_head(v_ref, "rows", h, n_kv, start, span))
            kc = _load_head(ck_ref, "heads", h, n_kv, 0, past).astype(BF16)
            vc = _with_ones(_load_head(cv_ref, "heads", h, n_kv, 0, past).astype(BF16))
            q = q_ref[h, :, q_rows, :].reshape(rows, HEAD_DIM)
            s_w = jnp.where(in_window, _qk(q, kw), NEG)
            s_c = _qk(q, kc)
            sink = jnp.zeros((rows, 1), F32)
            for g in range(GROUP):
                sink = jnp.where(row // BLOCK == g, sink_ref[h * GROUP + g] * LOG2E, sink)
            o = _softmax_sink_pv([s_w, s_c], [vw, vc], sink)
            for g in range(GROUP):
                head = h * GROUP + g
                o_ref[q_rows, head * HEAD_DIM:(head + 1) * HEAD_DIM] = (
                    o[g * BLOCK:(g + 1) * BLOCK].astype(o_ref.dtype))


def _attn_a_lat(q, k, v, ck, cv, sink, batch, seq, attach=None):
    n_kv, _, M, _ = q.shape
    n_blocks = seq // BLOCK
    per_step = ATTN_A_BLOCKS_PER_STEP if n_blocks % ATTN_A_BLOCKS_PER_STEP == 0 else 1
    n_steps = n_blocks // per_step
    assert n_blocks >= 3
    past = ck.shape[1] // n_kv
    width = n_kv * HEAD_DIM
    tq = per_step * BLOCK
    in_specs = [pl.BlockSpec(memory_space=pltpu.SMEM),
                pl.BlockSpec((n_kv, GROUP, tq, HEAD_DIM), lambda b, j: (0, 0, b * n_steps + j, 0)),
                _head_spec("rows", n_kv, seq, lambda b, j: b),
                _head_spec("rows", n_kv, seq, lambda b, j: b),
                pl.BlockSpec((None, past * n_kv, HEAD_DIM), lambda b, j: (b, 0, 0)),
                pl.BlockSpec((None, past * n_kv, HEAD_DIM), lambda b, j: (b, 0, 0))]
    riders = _Riders(batch * n_steps, lambda b, j: b * n_steps + j)
    if attach is not None:
        attach(riders)
    (o,), rider_out = _call_with_riders(
        functools.partial(_attn_a_lat_kernel, n_blocks=n_blocks, n_kv=n_kv, past=past, per_step=per_step),
        riders, len(in_specs), (batch, n_steps), in_specs,
        [pl.BlockSpec((tq, GROUP * width), lambda b, j: (b * n_steps + j, 0))],
        [jax.ShapeDtypeStruct((M, GROUP * width), BF16)],
        [sink, q, k, v, ck, cv], ("arbitrary",) * 2, "attn_a_lat")
    return o, rider_out


ATTN_B_Q_ROWS = 256
ATTN_B_TILES_PER_STEP = 2


def _attn_b_kernel(*refs, has_ctx, lam_init, n_kv, seq, past, v_layout, rider_kinds):
    ins, (o_ref,), run_riders = _split_refs(refs, 7 if has_ctx else 5, 1, rider_kinds)
    run_riders()
    if has_ctx:
        lam_ref, subg_ref, q_ref, kt_ref, v_ref, ckt_ref, cv_ref = ins
    else:
        lam_ref, subg_ref, q_ref, kt_ref, v_ref = ins
    h = pl.program_id(1)

    lv = lam_ref[...]
    lam = (jnp.exp(jnp.sum(lv[0:1] * lv[1:2], axis=-1, keepdims=True))
           - jnp.exp(jnp.sum(lv[2:3] * lv[3:4], axis=-1, keepdims=True)) + lam_init)

    kt = kt_ref[...].astype(F32)
    if v_layout == "rows":
        v = v_ref[...]
    else:
        v = _load_head(v_ref, "heads", h, n_kv, 0, seq).astype(BF16)
    if has_ctx:
        kt = jnp.concatenate([kt, ckt_ref[...]], axis=1)
        v = jnp.concatenate([v, _load_head(cv_ref, "heads", h, n_kv, 0, past).astype(BF16)], axis=0)
    vx = _with_ones(v)
    lo = lax.broadcasted_iota(jnp.int32, (HEAD_DIM, 1), 0) < DIFF_DIM
    kt0 = jnp.where(lo, kt, 0.0).astype(BF16)
    kt1 = jnp.where(lo, 0.0, kt).astype(BF16)

    tq = q_ref.shape[1]
    for r in range(0, tq, min(tq, ATTN_B_Q_ROWS)):
        q_rows = slice(r, r + min(tq, ATTN_B_Q_ROWS))
        for g in range(GROUP):
            q = q_ref[g, q_rows, :]
            maps = []
            for ktc in (kt0, kt1):
                s = jnp.dot(q, ktc, preferred_element_type=F32)
                ox = jnp.dot(_weights(s, jnp.max(s, axis=-1, keepdims=True)), vx, preferred_element_type=F32)
                maps.append(ox[:, :HEAD_DIM] / ox[:, HEAD_DIM:])
            o = maps[0] - lam * maps[1]
            ms = jnp.mean(o * o, axis=-1, keepdims=True)
            o = o * lax.rsqrt(ms + EPS) * subg_ref[...] * (1.0 - lam_init)
            o_ref[q_rows, g * HEAD_DIM:(g + 1) * HEAD_DIM] = o.astype(o_ref.dtype)


def _attn_b(q, kt, v, v_layout, ctx, lam_vecs, sub_g, lam_init, batch, seq, attach=None):
    n_kv, _, M, _ = q.shape
    tq = _tile(seq, ATTN_B_Q_ROWS * ATTN_B_TILES_PER_STEP)
    n_q = seq // tq
    has_ctx = ctx is not None
    n_steps = batch * n_kv * n_q
    if v_layout == "rows":
        v_spec = pl.BlockSpec((seq, HEAD_DIM), lambda b, h, j: (b, h))
    else:
        v_spec = _head_spec("heads", n_kv, seq, lambda b, h, j: b)
    in_specs = [pl.BlockSpec((4, DIFF_DIM), lambda b, h, j: (0, 0)),
                pl.BlockSpec((1, HEAD_DIM), lambda b, h, j: (0, 0)),
                pl.BlockSpec((None, GROUP, tq, HEAD_DIM), lambda b, h, j: (h, 0, b * n_q + j, 0)),
                pl.BlockSpec((None, HEAD_DIM, seq), lambda b, h, j: (b, h, 0)),
                v_spec]
    args = [lam_vecs, sub_g, q, kt, v]
    past = 0
    if has_ctx:
        past = ctx[0].shape[2]
        in_specs += [pl.BlockSpec((None, HEAD_DIM, past), lambda b, h, j: (b, h, 0)),
                     pl.BlockSpec((None, past * n_kv, HEAD_DIM), lambda b, h, j: (b, 0, 0))]
        args += list(ctx)
    out_specs = [pl.BlockSpec((tq, GROUP * HEAD_DIM), lambda b, h, j: (b * n_q + j, h))]
    out_shape = [jax.ShapeDtypeStruct((M, n_kv * GROUP * HEAD_DIM), BF16)]
    riders = _Riders(n_steps, lambda b, h, j: (b * n_kv + h) * n_q + j)
    if attach is not None:
        attach(riders)
    (o,), rider_out = _call_with_riders(
        functools.partial(_attn_b_kernel, has_ctx=has_ctx, lam_init=lam_init, n_kv=n_kv, seq=seq, past=past,
                          v_layout=v_layout),
        riders, len(in_specs), (batch, n_kv, n_q), in_specs, out_specs, out_shape, args,
        ("arbitrary",) * 3, "attn_b")
    return o, rider_out


def _oproj_kernel(oa_ref, ob_ref, wa_ref, wb_ref, x_ref, gt_ref, y_ref):
    acc = jnp.dot(oa_ref[...], wa_ref[...], preferred_element_type=F32)
    acc = acc + jnp.dot(ob_ref[...], wb_ref[...], preferred_element_type=F32)
    y_ref[...] = x_ref[...] + gt_ref[...] * acc


def _out_proj(o_a, o_b, w_out, x, mod, gt_idx, rows_per_mod):
    M, D = x.shape
    wa = o_a.shape[1]
    wb = o_b.shape[1]
    assert wa == wb and wa + wb == w_out.shape[0]
    tm = _tile(rows_per_mod, 1024) if mod.shape[0] > 1 else _tile(M, 1024)
    tn = _tile(D, 1024)
    return pl.pallas_call(
        _oproj_kernel,
        grid=(M // tm, D // tn),
        in_specs=[pl.BlockSpec((tm, wa), lambda m, n: (m, 0)),
                  pl.BlockSpec((tm, wb), lambda m, n: (m, 0)),
                  pl.BlockSpec((wa, tn), lambda m, n: (0, n)),
                  pl.BlockSpec((wb, tn), lambda m, n: (1, n)),
                  pl.BlockSpec((tm, tn), lambda m, n: (m, n)),
                  _mod_spec(mod, gt_idx, tn, tm, rows_per_mod, 0)],
        out_specs=pl.BlockSpec((tm, tn), lambda m, n: (m, n)),
        out_shape=jax.ShapeDtypeStruct((M, D), F32),
        compiler_params=_params("arbitrary", "arbitrary"),
        name="out_proj",
    )(o_a, o_b, w_out, w_out, x, mod)


def _fc1_kernel(*refs, rider_kinds):
    (a_ref, w_ref), (o_ref,), run_riders = _split_refs(refs, 2, 1, rider_kinds)
    run_riders()
    acc = jnp.dot(a_ref[...], w_ref[...], preferred_element_type=F32)
    r = jnp.maximum(acc, 0.0)
    o_ref[...] = (r * r).astype(o_ref.dtype)


def _fc1(h, w, attach=None):
    M, D = h.shape
    N = w.shape[1]
    tm = _tile(M, 1024)
    tn = _tile(N, 1024)
    n_n = N // tn
    riders = _Riders((M // tm) * n_n, lambda m, n: m * n_n + n)
    if attach is not None:
        attach(riders)
    (hid,), rider_out = _call_with_riders(
        _fc1_kernel, riders, 2, (M // tm, n_n),
        [pl.BlockSpec((tm, D), lambda m, n: (m, 0)), pl.BlockSpec((D, tn), lambda m, n: (0, n))],
        [pl.BlockSpec((tm, tn), lambda m, n: (m, n))],
        [jax.ShapeDtypeStruct((M, N), BF16)],
        [h, w], ("arbitrary",) * 2, "fc1")
    return hid, rider_out


def _fc2_kernel(a_ref, w_ref, x_ref, gt_ref, y_ref):
    k = pl.program_id(2)
    last = pl.num_programs(2) - 1

    def part():
        return jnp.dot(a_ref[...], w_ref[...], preferred_element_type=F32)

    @pl.when(k == 0)
    def _():
        y_ref[...] = part()

    @pl.when(jnp.logical_and(k > 0, k < last))
    def _():
        y_ref[...] += part()

    @pl.when(k == last)
    def _():
        y_ref[...] = x_ref[...] + gt_ref[...] * (y_ref[...] + part())


def _fc2(hid, w, x, mod, gt_idx, rows_per_mod):
    M, D = x.shape
    K = hid.shape[1]
    tm = _tile(rows_per_mod, 1024) if mod.shape[0] > 1 else _tile(M, 1024)
    tn = _tile(D, 1024)
    tk = _tile(K, 4096)
    assert K // tk >= 2
    return pl.pallas_call(
        _fc2_kernel,
        grid=(M // tm, D // tn, K // tk),
        in_specs=[pl.BlockSpec((tm, tk), lambda m, n, k: (m, k)),
                  pl.BlockSpec((tk, tn), lambda m, n, k: (k, n)),
                  pl.BlockSpec((tm, tn), lambda m, n, k: (m, n)),
                  _mod_spec(mod, gt_idx, tn, tm, rows_per_mod, 0)],
        out_specs=pl.BlockSpec((tm, tn), lambda m, n, k: (m, n)),
        out_shape=jax.ShapeDtypeStruct((M, D), F32),
        compiler_params=_params("arbitrary", "arbitrary", "arbitrary"),
        name="fc2",
    )(hid, w, x, mod)


def _rope_tables(seq, chunk):
    n = chunk // 4
    pos = jnp.arange(seq)
    row = (pos // GRID_W).astype(F32)
    col = (pos % GRID_W).astype(F32)
    freq = ROPE_BASE ** (-jnp.arange(n, dtype=F32) / n)
    ra = row[:, None] * freq
    ca = col[:, None] * freq
    cos = jnp.concatenate([jnp.cos(ra)] * 2 + [jnp.cos(ca)] * 2, axis=-1)
    sin = jnp.concatenate([-jnp.sin(ra), jnp.sin(ra), -jnp.sin(ca), jnp.sin(ca)], axis=-1)
    reps = HEAD_DIM // chunk
    return jnp.tile(cos, (1, reps)), jnp.tile(sin, (1, reps))


def _mixer(h, x, mod, w, l, batch, seq, n_kv_a, n_kv_b, ctx_kv, attach_a=None, attach_b=None):
    latent = ctx_kv is not None
    tab_a = _rope_tables(seq, HEAD_DIM) if latent else None
    tab_b = _rope_tables(seq, DIFF_DIM) if latent else None
    kv_dtype = BF16 if latent else F32
    kv_layout = "rows" if latent else "heads"

    gq_a = w["qnorm_a_g"].reshape(1, HEAD_DIM)
    gk_a = w["knorm_a_g"].reshape(1, HEAD_DIM)
    gq_b = jnp.tile(w["qnorm_b_g"], 2).reshape(1, HEAD_DIM)
    gk_b = jnp.tile(w["knorm_b_g"], 2).reshape(1, HEAD_DIM)
    assert n_kv_a == n_kv_b
    tm = _tile(seq, 1024) if latent else _tile(batch * seq, 512)
    q_a, k_a, v_a, q_b, kt_b, v_b = _in_proj(
        h, w["w_in"], (gq_a, gk_a, gq_b, gk_b), (tab_a, tab_b) if latent else None, seq, n_kv_a,
        kv_dtype, kv_layout, tm)

    lam_vecs = jnp.stack([w["lam_q1"], w["lam_k1"], w["lam_q2"], w["lam_k2"]]).astype(F32)
    lam_init = 0.8 - 0.6 * math.exp(-0.3 * l)
    sub_g = w["subln_g"].reshape(1, HEAD_DIM)
    w_out = w["w_out"]

    def attach_a_all(r):
        if attach_a is not None:
            attach_a(r)
        if w_out.dtype != BF16:
            r.add_cast(w_out)

    if latent:
        ck_a, cv_a, ckt_b, cv_b = ctx_kv
        o_a, ride_a = _attn_a_lat(q_a, k_a, v_a, ck_a, cv_a, w["sink_a"], batch, seq, attach_a_all)
        ctx_b = (ckt_b, cv_b)
    else:
        o_a, ride_a = _attn_a_ctx(q_a, k_a, v_a, w["sink_a"], batch, seq, attach_a_all)
        ctx_b = None
    if w_out.dtype != BF16:
        w_out = ride_a[-1]
    o_b, ride_b = _attn_b(q_b, kt_b, v_b, kv_layout, ctx_b, lam_vecs, sub_g, lam_init, batch, seq, attach_b)
    x = _out_proj(o_a, o_b, w_out, x, mod, 2, seq)
    return x, (k_a, v_a, kt_b, v_b), ride_a, ride_b


def _mlp(h2, x, mod, w_fc1, w_fc2, seq, attach=None):
    hid, ride = _fc1(h2, w_fc1, attach)
    return _fc2(hid, w_fc2, x, mod, 5, seq), ride


def kernel(x_prompt, x_sample, cache_a_k, cache_a_v, cache_b_k, cache_b_v, c, c_ctx, w_mod, b_mod, norm1_g, w_in, qnorm_a_g, knorm_a_g, qnorm_b_g, knorm_b_g, sink_a, lam_q1, lam_k1, lam_q2, lam_k2, subln_g, w_out, norm2_g, w_fc1, w_fc2):
    batch, seq, D = x_prompt.shape
    dbatch, dseq, _ = x_sample.shape
    depth = w_mod.shape[0]
    past = cache_a_k.shape[2]
    n_kv_a = cache_a_k.shape[3]
    n_kv_b = cache_b_k.shape[3]
    assert dseq % GRID_W == 0 and dseq % BLOCK == 0

    per_layer = dict(norm1_g=norm1_g, qnorm_a_g=qnorm_a_g, knorm_a_g=knorm_a_g, qnorm_b_g=qnorm_b_g,
                     knorm_b_g=knorm_b_g, sink_a=sink_a, lam_q1=lam_q1, lam_k1=lam_k1, lam_q2=lam_q2,
                     lam_k2=lam_k2, subln_g=subln_g, norm2_g=norm2_g)

    n_rows = -(-(dbatch + 1) // 8) * 8
    cvecs = jnp.zeros((n_rows, D), F32).at[:dbatch].set(c).at[dbatch].set(c_ctx)

    xp = x_prompt.reshape(batch * seq, D)
    xs = x_sample.reshape(dbatch * dseq, D)
    new_kv = []
    for l in range(depth):
        w = {k: v[l] for k, v in per_layer.items()}
        w["w_in"] = w_in[l].astype(BF16)
        w["w_out"] = w_out[l]
        mod = _modulation(cvecs, w_mod[l], b_mod[l])
        mod_ctx = mod[dbatch:dbatch + 1].reshape(1, 1, 6 * D)
        mod_lat = mod[:dbatch].reshape(dbatch, 1, 6 * D)

        xs_in = xs
        hp = _prenorm(xp, w["norm1_g"], mod_ctx, 0, 1, seq)
        xp, kv, (w_out_bf16,), (hs,) = _mixer(
            hp, xp, mod_ctx, w, l, batch, seq, n_kv_a, n_kv_b, None,
            attach_b=lambda r: r.add_norm(xs_in, w["norm1_g"], mod_lat, 0, 1, dseq))
        w["w_out"] = w_out_bf16
        new_kv.append(kv)
        ctx_kv = (cache_a_k[:, l].reshape(dbatch, past * n_kv_a, HEAD_DIM),
                  cache_a_v[:, l].reshape(dbatch, past * n_kv_a, HEAD_DIM),
                  jnp.transpose(cache_b_k[:, l], (0, 2, 3, 4, 1)).reshape(dbatch, n_kv_b * HEAD_DIM, past),
                  cache_b_v[:, l].reshape(dbatch, past * n_kv_b, HEAD_DIM))
        xp_mid = xp

        def attach_casts(r):
            r.add_cast(w_fc1[l])
            r.add_cast(w_fc2[l])

        xs, _, (hp2,), (fc1_bf16, fc2_bf16) = _mixer(
            hs, xs, mod_lat, w, l, dbatch, dseq, n_kv_a, n_kv_b, ctx_kv,
            attach_a=lambda r: r.add_norm(xp_mid, w["norm2_g"], mod_ctx, 3, 4, seq),
            attach_b=attach_casts)
        xs_mid = xs
        xp, (hs2,) = _mlp(hp2, xp, mod_ctx, fc1_bf16, fc2_bf16, seq,
                          attach=lambda r: r.add_norm(xs_mid, w["norm2_g"], mod_lat, 3, 4, dseq))
        xs, _ = _mlp(hs2, xs, mod_lat, fc1_bf16, fc2_bf16, dseq)

    def stack(i, to_cache_shape):
        return jnp.stack([to_cache_shape(kv[i]) for kv in new_kv], axis=1)

    def heads_a(t):
        return t.reshape(batch, seq, n_kv_a, HEAD_DIM)

    def heads_b(t):
        return t.reshape(batch, seq, n_kv_b, HEAD_DIM)

    def transposed_b(t):
        return jnp.transpose(t.reshape(batch, n_kv_b, 2, DIFF_DIM, seq), (0, 4, 1, 2, 3))

    return (xp.reshape(batch, seq, D), xs.reshape(dbatch, dseq, D),
            stack(0, heads_a), stack(1, heads_a), stack(2, transposed_b), stack(3, heads_b))
```

```python
import functools
import math

import jax
import jax.numpy as jnp
from jax import lax
from jax.experimental import pallas as pl
from jax.experimental.pallas import tpu as pltpu

F32 = jnp.float32
BF16 = jnp.bfloat16

HEAD_DIM = 128
BF16_SUBLANES = 16
DIFF_DIM = HEAD_DIM // 2
GROUP = 4
GRID_W = 64
WINDOW = 128
BLOCK = 128
ROPE_BASE = 10000.0
EPS = 1e-6
NEG = -1e30
LOG2E = math.log2(math.e)
V7X_VMEM_LIMIT_BYTES = 56 * 1024 * 1024


def _params(*sem):
    return pltpu.CompilerParams(dimension_semantics=sem, vmem_limit_bytes=V7X_VMEM_LIMIT_BYTES)


def _tile(dim, want):
    t = min(dim, want)
    assert dim % t == 0, (dim, want)
    return t


def _mod_kernel(c_ref, w_ref, b_ref, o_ref):
    c = c_ref[...]
    a = (c * jax.nn.sigmoid(c)).astype(BF16)
    acc = jnp.dot(a, w_ref[...].astype(BF16), preferred_element_type=F32)
    o_ref[...] = acc + b_ref[...]


def _modulation(cvecs, w_mod, b_mod):
    R, D = cvecs.shape
    N = w_mod.shape[1]
    tn = _tile(N, 512)
    return pl.pallas_call(
        _mod_kernel,
        grid=(N // tn,),
        in_specs=[pl.BlockSpec((R, D), lambda n: (0, 0)),
                  pl.BlockSpec((D, tn), lambda n: (0, n)),
                  pl.BlockSpec((1, tn), lambda n: (0, n))],
        out_specs=pl.BlockSpec((R, tn), lambda n: (0, n)),
        out_shape=jax.ShapeDtypeStruct((R, N), F32),
        compiler_params=_params("arbitrary"),
        name="modulation",
    )(cvecs, w_mod, b_mod.reshape(1, N))


def _mod_spec(mod, which, tn, tm, rows_per_mod, n_axis):
    per_batch = mod.shape[0] > 1
    D = mod.shape[2] // 6
    off = which * (D // tn)

    def index(m, *rest):
        n = rest[n_axis] if n_axis is not None else 0
        return ((m * tm) // rows_per_mod if per_batch else 0, 0, off + n)

    return pl.BlockSpec((None, 1, tn), index)


def _prenorm_kernel(x_ref, g_ref, sh_ref, sc_ref, o_ref):
    x = x_ref[...]
    ms = jnp.mean(x * x, axis=-1, keepdims=True)
    y = x * lax.rsqrt(ms + EPS) * g_ref[...]
    o_ref[...] = (y * (1.0 + sc_ref[...]) + sh_ref[...]).astype(o_ref.dtype)


def _prenorm(x, g, mod, sh_idx, sc_idx, rows_per_mod):
    M, D = x.shape
    tm = _tile(M, 256)
    assert rows_per_mod % tm == 0
    return pl.pallas_call(
        _prenorm_kernel,
        grid=(M // tm,),
        in_specs=[pl.BlockSpec((tm, D), lambda m: (m, 0)),
                  pl.BlockSpec((1, D), lambda m: (0, 0)),
                  _mod_spec(mod, sh_idx, D, tm, rows_per_mod, None),
                  _mod_spec(mod, sc_idx, D, tm, rows_per_mod, None)],
        out_specs=pl.BlockSpec((tm, D), lambda m: (m, 0)),
        out_shape=jax.ShapeDtypeStruct((M, D), BF16),
        compiler_params=_params("arbitrary"),
        name="prenorm",
    )(x, g.reshape(1, D), mod, mod)


class _Riders:
    N_INPUTS = {"cast": 1, "norm": 4}

    def __init__(self, n_steps, step_index):
        self.n_steps = n_steps
        self.step_index = step_index
        self.kinds, self.in_specs, self.args, self.out_specs, self.out_shapes = [], [], [], [], []

    def _slab_spec(self, rows, cols):
        slab = rows // self.n_steps
        assert rows % self.n_steps == 0 and slab % BF16_SUBLANES == 0, (rows, self.n_steps)
        return slab, pl.BlockSpec((slab, cols), lambda *g: (self.step_index(*g), 0))

    def add_cast(self, w):
        _, spec = self._slab_spec(*w.shape)
        self.kinds.append("cast")
        self.in_specs.append(spec)
        self.args.append(w)
        self.out_specs.append(spec)
        self.out_shapes.append(jax.ShapeDtypeStruct(w.shape, BF16))

    def add_norm(self, x, g, mod, sh_idx, sc_idx, rows_per_mod):
        M, D = x.shape
        slab, spec = self._slab_spec(M, D)
        assert rows_per_mod % slab == 0
        per_batch = mod.shape[0] > 1

        def mod_spec(which):
            return pl.BlockSpec((None, 1, D), lambda *g: (
                (self.step_index(*g) * slab) // rows_per_mod if per_batch else 0, 0, which))

        self.kinds.append("norm")
        self.in_specs += [spec, pl.BlockSpec((1, D), lambda *g: (0, 0)), mod_spec(sh_idx), mod_spec(sc_idx)]
        self.args += [x, g.reshape(1, D), mod, mod]
        self.out_specs.append(spec)
        self.out_shapes.append(jax.ShapeDtypeStruct((M, D), BF16))

    @staticmethod
    def run(kinds, in_refs, out_refs):
        i = 0
        for kind, dst in zip(kinds, out_refs):
            srcs = in_refs[i:i + _Riders.N_INPUTS[kind]]
            i += len(srcs)
            if kind == "cast":
                dst[...] = srcs[0][...].astype(dst.dtype)
            else:
                _prenorm_kernel(*srcs, dst)


def _split_refs(refs, n_in, n_out, rider_kinds):
    n_rin = sum(_Riders.N_INPUTS[k] for k in rider_kinds)
    host_out = refs[n_in + n_rin:n_in + n_rin + n_out]
    run = functools.partial(_Riders.run, rider_kinds, refs[n_in:n_in + n_rin], refs[n_in + n_rin + n_out:])
    return refs[:n_in], host_out, run


def _call_with_riders(kernel_fn, riders, n_in, grid, in_specs, out_specs, out_shapes, args, sem, name):
    kinds = ()
    if riders is not None:
        kinds = tuple(riders.kinds)
        in_specs = in_specs + riders.in_specs
        args = args + riders.args
        out_specs = out_specs + riders.out_specs
        out_shapes = out_shapes + riders.out_shapes
    assert len(in_specs) == n_in + sum(_Riders.N_INPUTS[k] for k in kinds)
    out = pl.pallas_call(
        functools.partial(kernel_fn, rider_kinds=kinds),
        grid=grid,
        in_specs=in_specs,
        out_specs=out_specs,
        out_shape=out_shapes,
        compiler_params=_params(*sem),
        name=name,
    )(*args)
    n_host_out = len(out) - len(kinds)
    return list(out[:n_host_out]), list(out[n_host_out:])


def _lane():
    return lax.broadcasted_iota(jnp.int32, (1, HEAD_DIM), 1)


def _head_norm(y, gain, chunk):
    y2 = y * y
    if chunk == HEAD_DIM:
        ms = jnp.mean(y2, axis=-1, keepdims=True)
    else:
        lo = _lane() < chunk
        s0 = jnp.sum(jnp.where(lo, y2, 0.0), axis=-1, keepdims=True)
        s1 = jnp.sum(jnp.where(lo, 0.0, y2), axis=-1, keepdims=True)
        ms = jnp.where(lo, s0, s1) * (1.0 / chunk)
    return y * lax.rsqrt(ms + EPS) * gain


def _rope(y, cos, sin, chunk):
    rot = chunk // 4
    first = (_lane() % (2 * rot)) < rot
    partner = jnp.where(first, pltpu.roll(y, HEAD_DIM - rot, 1), pltpu.roll(y, rot, 1))
    return y * cos + partner * sin


PROJ_ROW_CHUNK = 128


def _row_chunks(rows):
    step = min(rows, PROJ_ROW_CHUNK)
    return [slice(r, r + step) for r in range(0, rows, step)]


def _qproj_kernel(*refs, chunk, scale, rope):
    if rope:
        a_ref, w_ref, g_ref, cos_ref, sin_ref, o_ref = refs
    else:
        a_ref, w_ref, g_ref, o_ref = refs
    for rows in _row_chunks(a_ref.shape[0]):
        acc = jnp.dot(a_ref[rows, :], w_ref[...], preferred_element_type=F32)
        for h in range(o_ref.shape[0] * GROUP):
            y = _head_norm(acc[:, h * HEAD_DIM:(h + 1) * HEAD_DIM], g_ref[...], chunk)
            if rope:
                y = _rope(y, cos_ref[rows, :], sin_ref[rows, :], chunk)
            o_ref[h // GROUP, h % GROUP, rows, :] = (y * scale).astype(o_ref.dtype)


def _q_proj(h, w_in, col0, n_kv, gain, chunk, tables, seq):
    M, D = h.shape
    tm = _tile(seq, 1024) if tables is not None else _tile(M, 1024)
    kv_per_step = 2 if n_kv % 2 == 0 and col0 % (2 * GROUP * HEAD_DIM) == 0 else 1
    tn = kv_per_step * GROUP * HEAD_DIM
    assert col0 % tn == 0
    rope = tables is not None
    in_specs = [pl.BlockSpec((tm, D), lambda m, n: (m, 0)),
                pl.BlockSpec((D, tn), lambda m, n: (0, col0 // tn + n)),
                pl.BlockSpec((1, HEAD_DIM), lambda m, n: (0, 0))]
    args = [h, w_in, gain]
    if rope:
        per_seq = seq // tm
        in_specs += [pl.BlockSpec((tm, HEAD_DIM), lambda m, n: (m % per_seq, 0))] * 2
        args += list(tables)
    return pl.pallas_call(
        functools.partial(_qproj_kernel, chunk=chunk, scale=chunk ** -0.5 * LOG2E, rope=rope),
        grid=(M // tm, n_kv // kv_per_step),
        in_specs=in_specs,
        out_specs=pl.BlockSpec((kv_per_step, GROUP, tm, HEAD_DIM), lambda m, n: (n, 0, m, 0)),
        out_shape=jax.ShapeDtypeStruct((n_kv, GROUP, M, HEAD_DIM), BF16),
        compiler_params=_params("arbitrary", "arbitrary"),
        name="q_proj",
    )(*args)


def _kvproj_kernel(*refs, chunk, rope, n_kv, seq, k_layout, v_layout):
    if rope:
        a_ref, w_ref, g_ref, cos_ref, sin_ref, k_ref, v_ref = refs
    else:
        a_ref, w_ref, g_ref, k_ref, v_ref = refs
    width = n_kv * HEAD_DIM
    for rows in _row_chunks(a_ref.shape[0]):
        acc = jnp.dot(a_ref[rows, :], w_ref[...], preferred_element_type=F32)
        for h in range(n_kv):
            sl = slice(h * HEAD_DIM, (h + 1) * HEAD_DIM)
            y = _head_norm(acc[:, sl], g_ref[...], chunk)
            if rope:
                y = _rope(y, cos_ref[rows, :], sin_ref[rows, :], chunk)
            _store_head(k_ref, k_layout, y, h, n_kv, rows, seq)
            _store_head(v_ref, v_layout, acc[:, width + h * HEAD_DIM:width + (h + 1) * HEAD_DIM], h, n_kv, rows, seq)


def _store_head(ref, layout, y, h, n_kv, rows, seq):
    y = y.astype(ref.dtype) if layout != "transposed" else y
    if layout == "rows":
        ref[rows, h * HEAD_DIM:(h + 1) * HEAD_DIM] = y
    elif layout == "heads":
        ref[pl.ds(rows.start * n_kv + h, rows.stop - rows.start, stride=n_kv), :] = y
    else:
        s, t0 = divmod(rows.start, seq)
        ref[s, h * HEAD_DIM:(h + 1) * HEAD_DIM, t0:t0 + rows.stop - rows.start] = y.T.astype(ref.dtype)


def _load_head(ref, layout, h, n_kv, t0, n):
    if layout == "rows":
        return ref[pl.ds(t0, n), h * HEAD_DIM:(h + 1) * HEAD_DIM]
    return ref[pl.ds(t0 * n_kv + h, n, stride=n_kv), :]


def _head_spec(layout, n_kv, rows, index):
    shape = (rows, n_kv * HEAD_DIM) if layout == "rows" else (rows * n_kv, HEAD_DIM)
    return pl.BlockSpec(shape, lambda *g: (index(*g), 0))


def _kv_proj(h, w_in, col0, n_kv, gain, chunk, tables, seq, out_dtype, k_layout, v_layout):
    M, D = h.shape
    tm = _tile(seq, 1024) if tables is not None else _tile(M, 1024)
    width = n_kv * HEAD_DIM
    tn = 2 * width
    assert col0 % tn == 0 and tm % seq == 0 and seq % PROJ_ROW_CHUNK == 0
    rope = tables is not None

    def out(layout):
        if layout == "transposed":
            return (pl.BlockSpec((tm // seq, width, seq), lambda m: (m, 0, 0)),
                    jax.ShapeDtypeStruct((M // seq, width, seq), out_dtype))
        shape = (M, width) if layout == "rows" else (M * n_kv, HEAD_DIM)
        return _head_spec(layout, n_kv, tm, lambda m: m), jax.ShapeDtypeStruct(shape, out_dtype)

    (k_spec, k_shape), (v_spec, v_shape) = out(k_layout), out(v_layout)
    in_specs = [pl.BlockSpec((tm, D), lambda m: (m, 0)),
                pl.BlockSpec((D, tn), lambda m: (0, col0 // tn)),
                pl.BlockSpec((1, HEAD_DIM), lambda m: (0, 0))]
    args = [h, w_in, gain]
    if rope:
        per_seq = seq // tm
        in_specs += [pl.BlockSpec((tm, HEAD_DIM), lambda m: (m % per_seq, 0))] * 2
        args += list(tables)
    return pl.pallas_call(
        functools.partial(_kvproj_kernel, chunk=chunk, rope=rope, n_kv=n_kv, seq=seq,
                          k_layout=k_layout, v_layout=v_layout),
        grid=(M // tm,),
        in_specs=in_specs,
        out_specs=[k_spec, v_spec],
        out_shape=[k_shape, v_shape],
        compiler_params=_params("arbitrary"),
        name="kv_proj",
    )(*args)


def _with_ones(v):
    return jnp.concatenate([v, jnp.ones_like(v)], axis=1)


def _weights(s, m):
    return jnp.exp2((s - m).astype(BF16))


def _softmax_sink_pv(scores, values_ones, sink):
    m = sink
    for s in scores:
        m = jnp.maximum(m, jnp.max(s, axis=-1, keepdims=True))
    ox = None
    for s, vx in zip(scores, values_ones):
        pv = jnp.dot(_weights(s, m), vx, preferred_element_type=F32)
        ox = pv if ox is None else ox + pv
    denom = ox[:, HEAD_DIM:] + jnp.exp2(sink - m)
    return ox[:, :HEAD_DIM] / denom


def _qk(q, k):
    return lax.dot_general(q, k, (((1,), (1,)), ((), ())), preferred_element_type=F32)


def _attn_a_ctx_kernel(*refs, n_kv, seq, rider_kinds):
    (sink_ref, q_ref, k_ref, v_ref), (o_ref,), run_riders = _split_refs(refs, 4, 1, rider_kinds)
    run_riders()
    for h in range(n_kv):
        k = _load_head(k_ref, "heads", h, n_kv, 0, seq).astype(BF16)
        vx = _with_ones(_load_head(v_ref, "heads", h, n_kv, 0, seq).astype(BF16))
        for g in range(GROUP):
            head = h * GROUP + g
            sink = jnp.full((1, 1), sink_ref[head] * LOG2E, F32)
            o = _softmax_sink_pv([_qk(q_ref[h, g], k)], [vx], sink)
            o_ref[:, head * HEAD_DIM:(head + 1) * HEAD_DIM] = o.astype(o_ref.dtype)


def _attn_a_ctx(q, k, v, sink, batch, seq, attach=None):
    n_kv, _, M, _ = q.shape
    width = n_kv * HEAD_DIM
    in_specs = [pl.BlockSpec(memory_space=pltpu.SMEM),
                pl.BlockSpec((n_kv, GROUP, seq, HEAD_DIM), lambda b: (0, 0, b, 0)),
                _head_spec("heads", n_kv, seq, lambda b: b),
                _head_spec("heads", n_kv, seq, lambda b: b)]
    riders = _Riders(batch, lambda b: b)
    if attach is not None:
        attach(riders)
    (o,), rider_out = _call_with_riders(
        functools.partial(_attn_a_ctx_kernel, n_kv=n_kv, seq=seq), riders, len(in_specs), (batch,), in_specs,
        [pl.BlockSpec((seq, GROUP * width), lambda b: (b, 0))],
        [jax.ShapeDtypeStruct((M, GROUP * width), BF16)],
        [sink, q, k, v], ("arbitrary",), "attn_a_ctx")
    return o, rider_out


ATTN_A_BLOCKS_PER_STEP = 4


def _attn_a_lat_kernel(*refs, n_blocks, n_kv, past, per_step, rider_kinds):
    (sink_ref, q_ref, k_ref, v_ref, ck_ref, cv_ref), (o_ref,), run_riders = _split_refs(refs, 6, 1, rider_kinds)
    run_riders()
    rows = GROUP * BLOCK
    span = 3 * BLOCK
    row = lax.broadcasted_iota(jnp.int32, (rows, 1), 0)
    lane = lax.broadcasted_iota(jnp.int32, (1, span), 1)

    for u in range(per_step):
        j = pl.program_id(1) * per_step + u
        start = pl.multiple_of(jnp.clip(j - 1, 0, n_blocks - 3) * BLOCK, BLOCK)
        in_window = jnp.abs(start + lane - (j * BLOCK + row % BLOCK)) <= WINDOW
        q_rows = slice(u * BLOCK, (u + 1) * BLOCK)
        for h in range(n_kv):
            kw = _load_head(k_ref, "rows", h, n_kv, start, span)
            vw = _with_ones(_load_head(v_ref, "rows", h, n_kv, start, span))
            kc = _load_head(ck_ref, "heads", h, n_kv, 0, past).astype(BF16)
            vc = _with_ones(_load_head(cv_ref, "heads", h, n_kv, 0, past).astype(BF16))
            q = q_ref[h, :, q_rows, :].reshape(rows, HEAD_DIM)
            s_w = jnp.where(in_window, _qk(q, kw), NEG)
            s_c = _qk(q, kc)
            sink = jnp.zeros((rows, 1), F32)
            for g in range(GROUP):
                sink = jnp.where(row // BLOCK == g, sink_ref[h * GROUP + g] * LOG2E, sink)
            o = _softmax_sink_pv([s_w, s_c], [vw, vc], sink)
            for g in range(GROUP):
                head = h * GROUP + g
                o_ref[q_rows, head * HEAD_DIM:(head + 1) * HEAD_DIM] = (
                    o[g * BLOCK:(g + 1) * BLOCK].astype(o_ref.dtype))


def _attn_a_lat(q, k, v, ck, cv, sink, batch, seq, attach=None):
    n_kv, _, M, _ = q.shape
    n_blocks = seq // BLOCK
    per_step = ATTN_A_BLOCKS_PER_STEP if n_blocks % ATTN_A_BLOCKS_PER_STEP == 0 else 1
    n_steps = n_blocks // per_step
    assert n_blocks >= 3
    past = ck.shape[1] // n_kv
    width = n_kv * HEAD_DIM
    tq = per_step * BLOCK
    in_specs = [pl.BlockSpec(memory_space=pltpu.SMEM),
                pl.BlockSpec((n_kv, GROUP, tq, HEAD_DIM), lambda b, j: (0, 0, b * n_steps + j, 0)),
                _head_spec("rows", n_kv, seq, lambda b, j: b),
                _head_spec("rows", n_kv, seq, lambda b, j: b),
                pl.BlockSpec((None, past * n_kv, HEAD_DIM), lambda b, j: (b, 0, 0)),
                pl.BlockSpec((None, past * n_kv, HEAD_DIM), lambda b, j: (b, 0, 0))]
    riders = _Riders(batch * n_steps, lambda b, j: b * n_steps + j)
    if attach is not None:
        attach(riders)
    (o,), rider_out = _call_with_riders(
        functools.partial(_attn_a_lat_kernel, n_blocks=n_blocks, n_kv=n_kv, past=past, per_step=per_step),
        riders, len(in_specs), (batch, n_steps), in_specs,
        [pl.BlockSpec((tq, GROUP * width), lambda b, j: (b * n_steps + j, 0))],
        [jax.ShapeDtypeStruct((M, GROUP * width), BF16)],
        [sink, q, k, v, ck, cv], ("arbitrary",) * 2, "attn_a_lat")
    return o, rider_out


ATTN_B_Q_ROWS = 256
ATTN_B_TILES_PER_STEP = 2


def _attn_b_kernel(*refs, has_ctx, lam_init, n_kv, seq, past, v_layout, rider_kinds):
    ins, (o_ref,), run_riders = _split_refs(refs, 7 if has_ctx else 5, 1, rider_kinds)
    run_riders()
    if has_ctx:
        lam_ref, subg_ref, q_ref, kt_ref, v_ref, ckt_ref, cv_ref = ins
    else:
        lam_ref, subg_ref, q_ref, kt_ref, v_ref = ins
    h = pl.program_id(1)

    lv = lam_ref[...]
    lam = (jnp.exp(jnp.sum(lv[0:1] * lv[1:2], axis=-1, keepdims=True))
           - jnp.exp(jnp.sum(lv[2:3] * lv[3:4], axis=-1, keepdims=True)) + lam_init)

    kt = kt_ref[...].astype(F32)
    if v_layout == "rows":
        v = v_ref[...]
    else:
        v = _load_head(v_ref, "heads", h, n_kv, 0, seq).astype(BF16)
    if has_ctx:
        kt = jnp.concatenate([kt, ckt_ref[...]], axis=1)
        v = jnp.concatenate([v, _load_head(cv_ref, "heads", h, n_kv, 0, past).astype(BF16)], axis=0)
    vx = _with_ones(v)
    lo = lax.broadcasted_iota(jnp.int32, (HEAD_DIM, 1), 0) < DIFF_DIM
    kt0 = jnp.where(lo, kt, 0.0).astype(BF16)
    kt1 = jnp.where(lo, 0.0, kt).astype(BF16)

    tq = q_ref.shape[1]
    for r in range(0, tq, min(tq, ATTN_B_Q_ROWS)):
        q_rows = slice(r, r + min(tq, ATTN_B_Q_ROWS))
        for g in range(GROUP):
            q = q_ref[g, q_rows, :]
            maps = []
            for ktc in (kt0, kt1):
                s = jnp.dot(q, ktc, preferred_element_type=F32)
                ox = jnp.dot(_weights(s, jnp.max(s, axis=-1, keepdims=True)), vx, preferred_element_type=F32)
                maps.append(ox[:, :HEAD_DIM] / ox[:, HEAD_DIM:])
            o = maps[0] - lam * maps[1]
            ms = jnp.mean(o * o, axis=-1, keepdims=True)
            o = o * lax.rsqrt(ms + EPS) * subg_ref[...] * (1.0 - lam_init)
            o_ref[q_rows, g * HEAD_DIM:(g + 1) * HEAD_DIM] = o.astype(o_ref.dtype)


def _attn_b(q, kt, v, v_layout, ctx, lam_vecs, sub_g, lam_init, batch, seq, attach=None):
    n_kv, _, M, _ = q.shape
    tq = _tile(seq, ATTN_B_Q_ROWS * ATTN_B_TILES_PER_STEP)
    n_q = seq // tq
    has_ctx = ctx is not None
    n_steps = batch * n_kv * n_q
    if v_layout == "rows":
        v_spec = pl.BlockSpec((seq, HEAD_DIM), lambda b, h, j: (b, h))
    else:
        v_spec = _head_spec("heads", n_kv, seq, lambda b, h, j: b)
    in_specs = [pl.BlockSpec((4, DIFF_DIM), lambda b, h, j: (0, 0)),
                pl.BlockSpec((1, HEAD_DIM), lambda b, h, j: (0, 0)),
                pl.BlockSpec((None, GROUP, tq, HEAD_DIM), lambda b, h, j: (h, 0, b * n_q + j, 0)),
                pl.BlockSpec((None, HEAD_DIM, seq), lambda b, h, j: (b, h, 0)),
                v_spec]
    args = [lam_vecs, sub_g, q, kt, v]
    past = 0
    if has_ctx:
        past = ctx[0].shape[2]
        in_specs += [pl.BlockSpec((None, HEAD_DIM, past), lambda b, h, j: (b, h, 0)),
                     pl.BlockSpec((None, past * n_kv, HEAD_DIM), lambda b, h, j: (b, 0, 0))]
        args += list(ctx)
    out_specs = [pl.BlockSpec((tq, GROUP * HEAD_DIM), lambda b, h, j: (b * n_q + j, h))]
    out_shape = [jax.ShapeDtypeStruct((M, n_kv * GROUP * HEAD_DIM), BF16)]
    riders = _Riders(n_steps, lambda b, h, j: (b * n_kv + h) * n_q + j)
    if attach is not None:
        attach(riders)
    (o,), rider_out = _call_with_riders(
        functools.partial(_attn_b_kernel, has_ctx=has_ctx, lam_init=lam_init, n_kv=n_kv, seq=seq, past=past,
                          v_layout=v_layout),
        riders, len(in_specs), (batch, n_kv, n_q), in_specs, out_specs, out_shape, args,
        ("arbitrary",) * 3, "attn_b")
    return o, rider_out


def _oproj_kernel(oa_ref, ob_ref, wa_ref, wb_ref, x_ref, gt_ref, y_ref):
    acc = jnp.dot(oa_ref[...], wa_ref[...], preferred_element_type=F32)
    acc = acc + jnp.dot(ob_ref[...], wb_ref[...], preferred_element_type=F32)
    y_ref[...] = x_ref[...] + gt_ref[...] * acc


def _out_proj(o_a, o_b, w_out, x, mod, gt_idx, rows_per_mod):
    M, D = x.shape
    wa = o_a.shape[1]
    wb = o_b.shape[1]
    assert wa == wb and wa + wb == w_out.shape[0]
    tm = _tile(rows_per_mod, 1024) if mod.shape[0] > 1 else _tile(M, 1024)
    tn = _tile(D, 1024)
    return pl.pallas_call(
        _oproj_kernel,
        grid=(M // tm, D // tn),
        in_specs=[pl.BlockSpec((tm, wa), lambda m, n: (m, 0)),
                  pl.BlockSpec((tm, wb), lambda m, n: (m, 0)),
                  pl.BlockSpec((wa, tn), lambda m, n: (0, n)),
                  pl.BlockSpec((wb, tn), lambda m, n: (1, n)),
                  pl.BlockSpec((tm, tn), lambda m, n: (m, n)),
                  _mod_spec(mod, gt_idx, tn, tm, rows_per_mod, 0)],
        out_specs=pl.BlockSpec((tm, tn), lambda m, n: (m, n)),
        out_shape=jax.ShapeDtypeStruct((M, D), F32),
        compiler_params=_params("arbitrary", "arbitrary"),
        name="out_proj",
    )(o_a, o_b, w_out, w_out, x, mod)


def _fc1_kernel(*refs, rider_kinds):
    (a_ref, w_ref), (o_ref,), run_riders = _split_refs(refs, 2, 1, rider_kinds)
    run_riders()
    acc = jnp.dot(a_ref[...], w_ref[...], preferred_element_type=F32)
    r = jnp.maximum(acc, 0.0)
    o_ref[...] = (r * r).astype(o_ref.dtype)


def _fc1(h, w, attach=None):
    M, D = h.shape
    N = w.shape[1]
    tm = _tile(M, 1024)
    tn = _tile(N, 1024)
    n_n = N // tn
    riders = _Riders((M // tm) * n_n, lambda m, n: m * n_n + n)
    if attach is not None:
        attach(riders)
    (hid,), rider_out = _call_with_riders(
        _fc1_kernel, riders, 2, (M // tm, n_n),
        [pl.BlockSpec((tm, D), lambda m, n: (m, 0)), pl.BlockSpec((D, tn), lambda m, n: (0, n))],
        [pl.BlockSpec((tm, tn), lambda m, n: (m, n))],
        [jax.ShapeDtypeStruct((M, N), BF16)],
        [h, w], ("arbitrary",) * 2, "fc1")
    return hid, rider_out


def _fc2_kernel(a_ref, w_ref, x_ref, gt_ref, y_ref):
    k = pl.program_id(2)
    last = pl.num_programs(2) - 1

    def part():
        return jnp.dot(a_ref[...], w_ref[...], preferred_element_type=F32)

    @pl.when(k == 0)
    def _():
        y_ref[...] = part()

    @pl.when(jnp.logical_and(k > 0, k < last))
    def _():
        y_ref[...] += part()

    @pl.when(k == last)
    def _():
        y_ref[...] = x_ref[...] + gt_ref[...] * (y_ref[...] + part())


def _fc2(hid, w, x, mod, gt_idx, rows_per_mod):
    M, D = x.shape
    K = hid.shape[1]
    tm = _tile(rows_per_mod, 1024) if mod.shape[0] > 1 else _tile(M, 1024)
    tn = _tile(D, 1024)
    tk = _tile(K, 4096)
    assert K // tk >= 2
    return pl.pallas_call(
        _fc2_kernel,
        grid=(M // tm, D // tn, K // tk),
        in_specs=[pl.BlockSpec((tm, tk), lambda m, n, k: (m, k)),
                  pl.BlockSpec((tk, tn), lambda m, n, k: (k, n)),
                  pl.BlockSpec((tm, tn), lambda m, n, k: (m, n)),
                  _mod_spec(mod, gt_idx, tn, tm, rows_per_mod, 0)],
        out_specs=pl.BlockSpec((tm, tn), lambda m, n, k: (m, n)),
        out_shape=jax.ShapeDtypeStruct((M, D), F32),
        compiler_params=_params("arbitrary", "arbitrary", "arbitrary"),
        name="fc2",
    )(hid, w, x, mod)


def _rope_tables(seq, chunk):
    n = chunk // 4
    pos = jnp.arange(seq)
    row = (pos // GRID_W).astype(F32)
    col = (pos % GRID_W).astype(F32)
    freq = ROPE_BASE ** (-jnp.arange(n, dtype=F32) / n)
    ra = row[:, None] * freq
    ca = col[:, None] * freq
    cos = jnp.concatenate([jnp.cos(ra)] * 2 + [jnp.cos(ca)] * 2, axis=-1)
    sin = jnp.concatenate([-jnp.sin(ra), jnp.sin(ra), -jnp.sin(ca), jnp.sin(ca)], axis=-1)
    reps = HEAD_DIM // chunk
    return jnp.tile(cos, (1, reps)), jnp.tile(sin, (1, reps))


def _mixer(h, x, mod, w, l, batch, seq, n_kv_a, n_kv_b, ctx_kv, attach_a=None, attach_b=None):
    latent = ctx_kv is not None
    wa = n_kv_a * GROUP * HEAD_DIM
    ka = n_kv_a * HEAD_DIM
    wb = n_kv_b * GROUP * HEAD_DIM
    tab_a = _rope_tables(seq, HEAD_DIM) if latent else None
    tab_b = _rope_tables(seq, DIFF_DIM) if latent else None
    kv_dtype = BF16 if latent else F32
    kv_layout = "rows" if latent else "heads"

    gq_a = w["qnorm_a_g"].reshape(1, HEAD_DIM)
    gk_a = w["knorm_a_g"].reshape(1, HEAD_DIM)
    gq_b = jnp.tile(w["qnorm_b_g"], 2).reshape(1, HEAD_DIM)
    gk_b = jnp.tile(w["knorm_b_g"], 2).reshape(1, HEAD_DIM)
    q_a = _q_proj(h, w["w_in"], 0, n_kv_a, gq_a, HEAD_DIM, tab_a, seq)
    k_a, v_a = _kv_proj(h, w["w_in"], wa, n_kv_a, gk_a, HEAD_DIM, tab_a, seq, kv_dtype, kv_layout, kv_layout)
    q_b = _q_proj(h, w["w_in"], wa + 2 * ka, n_kv_b, gq_b, DIFF_DIM, tab_b, seq)
    kt_b, v_b = _kv_proj(h, w["w_in"], wa + 2 * ka + wb, n_kv_b, gk_b, DIFF_DIM, tab_b, seq, kv_dtype,
                         "transposed", kv_layout)

    lam_vecs = jnp.stack([w["lam_q1"], w["lam_k1"], w["lam_q2"], w["lam_k2"]]).astype(F32)
    lam_init = 0.8 - 0.6 * math.exp(-0.3 * l)
    sub_g = w["subln_g"].reshape(1, HEAD_DIM)
    w_out = w["w_out"]

    def attach_a_all(r):
        if attach_a is not None:
            attach_a(r)
        if w_out.dtype != BF16:
            r.add_cast(w_out)

    if latent:
        ck_a, cv_a, ckt_b, cv_b = ctx_kv
        o_a, ride_a = _attn_a_lat(q_a, k_a, v_a, ck_a, cv_a, w["sink_a"], batch, seq, attach_a_all)
        ctx_b = (ckt_b, cv_b)
    else:
        o_a, ride_a = _attn_a_ctx(q_a, k_a, v_a, w["sink_a"], batch, seq, attach_a_all)
        ctx_b = None
    if w_out.dtype != BF16:
        w_out = ride_a[-1]
    o_b, ride_b = _attn_b(q_b, kt_b, v_b, kv_layout, ctx_b, lam_vecs, sub_g, lam_init, batch, seq, attach_b)
    x = _out_proj(o_a, o_b, w_out, x, mod, 2, seq)
    return x, (k_a, v_a, kt_b, v_b), ride_a, ride_b


def _mlp(h2, x, mod, w_fc1, w_fc2, seq, attach=None):
    hid, ride = _fc1(h2, w_fc1, attach)
    return _fc2(hid, w_fc2, x, mod, 5, seq), ride


def kernel(x_prompt, x_sample, cache_a_k, cache_a_v, cache_b_k, cache_b_v, c, c_ctx, w_mod, b_mod, norm1_g, w_in, qnorm_a_g, knorm_a_g, qnorm_b_g, knorm_b_g, sink_a, lam_q1, lam_k1, lam_q2, lam_k2, subln_g, w_out, norm2_g, w_fc1, w_fc2):
    batch, seq, D = x_prompt.shape
    dbatch, dseq, _ = x_sample.shape
    depth = w_mod.shape[0]
    past = cache_a_k.shape[2]
    n_kv_a = cache_a_k.shape[3]
    n_kv_b = cache_b_k.shape[3]
    assert dseq % GRID_W == 0 and dseq % BLOCK == 0

    per_layer = dict(norm1_g=norm1_g, qnorm_a_g=qnorm_a_g, knorm_a_g=knorm_a_g, qnorm_b_g=qnorm_b_g,
                     knorm_b_g=knorm_b_g, sink_a=sink_a, lam_q1=lam_q1, lam_k1=lam_k1, lam_q2=lam_q2,
                     lam_k2=lam_k2, subln_g=subln_g, norm2_g=norm2_g)

    n_rows = -(-(dbatch + 1) // 8) * 8
    cvecs = jnp.zeros((n_rows, D), F32).at[:dbatch].set(c).at[dbatch].set(c_ctx)

    xp = x_prompt.reshape(batch * seq, D)
    xs = x_sample.reshape(dbatch * dseq, D)
    new_kv = []
    for l in range(depth):
        w = {k: v[l] for k, v in per_layer.items()}
        w["w_in"] = w_in[l].astype(BF16)
        w["w_out"] = w_out[l]
        mod = _modulation(cvecs, w_mod[l], b_mod[l])
        mod_ctx = mod[dbatch:dbatch + 1].reshape(1, 1, 6 * D)
        mod_lat = mod[:dbatch].reshape(dbatch, 1, 6 * D)

        xs_in = xs
        hp = _prenorm(xp, w["norm1_g"], mod_ctx, 0, 1, seq)
        xp, kv, (w_out_bf16,), (hs,) = _mixer(
            hp, xp, mod_ctx, w, l, batch, seq, n_kv_a, n_kv_b, None,
            attach_b=lambda r: r.add_norm(xs_in, w["norm1_g"], mod_lat, 0, 1, dseq))
        w["w_out"] = w_out_bf16
        new_kv.append(kv)
        ctx_kv = (cache_a_k[:, l].reshape(dbatch, past * n_kv_a, HEAD_DIM),
                  cache_a_v[:, l].reshape(dbatch, past * n_kv_a, HEAD_DIM),
                  jnp.transpose(cache_b_k[:, l], (0, 2, 3, 4, 1)).reshape(dbatch, n_kv_b * HEAD_DIM, past),
                  cache_b_v[:, l].reshape(dbatch, past * n_kv_b, HEAD_DIM))
        xp_mid = xp

        def attach_casts(r):
            r.add_cast(w_fc1[l])
            r.add_cast(w_fc2[l])

        xs, _, (hp2,), (fc1_bf16, fc2_bf16) = _mixer(
            hs, xs, mod_lat, w, l, dbatch, dseq, n_kv_a, n_kv_b, ctx_kv,
            attach_a=lambda r: r.add_norm(xp_mid, w["norm2_g"], mod_ctx, 3, 4, seq),
            attach_b=attach_casts)
        xs_mid = xs
        xp, (hs2,) = _mlp(hp2, xp, mod_ctx, fc1_bf16, fc2_bf16, seq,
                          attach=lambda r: r.add_norm(xs_mid, w["norm2_g"], mod_lat, 3, 4, dseq))
        xs, _ = _mlp(hs2, xs, mod_lat, fc1_bf16, fc2_bf16, dseq)

    def stack(i, to_cache_shape):
        return jnp.stack([to_cache_shape(kv[i]) for kv in new_kv], axis=1)

    def heads_a(t):
        return t.reshape(batch, seq, n_kv_a, HEAD_DIM)

    def heads_b(t):
        return t.reshape(batch, seq, n_kv_b, HEAD_DIM)

    def transposed_b(t):
        return jnp.transpose(t.reshape(batch, n_kv_b, 2, DIFF_DIM, seq), (0, 4, 1, 2, 3))

    return (xp.reshape(batch, seq, D), xs.reshape(dbatch, dseq, D),
            stack(0, heads_a), stack(1, heads_a), stack(2, transposed_b), stack(3, heads_b))
```

```python
import functools
import math

import jax
import jax.numpy as jnp
from jax import lax
from jax.experimental import pallas as pl
from jax.experimental.pallas import tpu as pltpu

F32 = jnp.float32
BF16 = jnp.bfloat16

HEAD_DIM = 128
BF16_SUBLANES = 16
DIFF_DIM = HEAD_DIM // 2
GROUP = 4
GRID_W = 64
WINDOW = 128
BLOCK = 128
ROPE_BASE = 10000.0
EPS = 1e-6
NEG = -1e30
LOG2E = math.log2(math.e)
V7X_VMEM_LIMIT_BYTES = 56 * 1024 * 1024


def _params(*sem):
    return pltpu.CompilerParams(dimension_semantics=sem, vmem_limit_bytes=V7X_VMEM_LIMIT_BYTES)


def _tile(dim, want):
    t = min(dim, want)
    assert dim % t == 0, (dim, want)
    return t


def _mod_kernel(c_ref, w_ref, b_ref, o_ref):
    c = c_ref[...]
    a = (c * jax.nn.sigmoid(c)).astype(BF16)
    acc = jnp.dot(a, w_ref[...].astype(BF16), preferred_element_type=F32)
    o_ref[...] = acc + b_ref[...]


def _modulation(cvecs, w_mod, b_mod):
    R, D = cvecs.shape
    N = w_mod.shape[1]
    tn = _tile(N, 512)
    return pl.pallas_call(
        _mod_kernel,
        grid=(N // tn,),
        in_specs=[pl.BlockSpec((R, D), lambda n: (0, 0)),
                  pl.BlockSpec((D, tn), lambda n: (0, n)),
                  pl.BlockSpec((1, tn), lambda n: (0, n))],
        out_specs=pl.BlockSpec((R, tn), lambda n: (0, n)),
        out_shape=jax.ShapeDtypeStruct((R, N), F32),
        compiler_params=_params("arbitrary"),
        name="modulation",
    )(cvecs, w_mod, b_mod.reshape(1, N))


def _mod_spec(mod, which, tn, tm, rows_per_mod, n_axis):
    per_batch = mod.shape[0] > 1
    D = mod.shape[2] // 6
    off = which * (D // tn)

    def index(m, *rest):
        n = rest[n_axis] if n_axis is not None else 0
        return ((m * tm) // rows_per_mod if per_batch else 0, 0, off + n)

    return pl.BlockSpec((None, 1, tn), index)


def _prenorm_kernel(x_ref, g_ref, sh_ref, sc_ref, o_ref):
    x = x_ref[...]
    ms = jnp.mean(x * x, axis=-1, keepdims=True)
    y = x * lax.rsqrt(ms + EPS) * g_ref[...]
    o_ref[...] = (y * (1.0 + sc_ref[...]) + sh_ref[...]).astype(o_ref.dtype)


def _prenorm(x, g, mod, sh_idx, sc_idx, rows_per_mod):
    M, D = x.shape
    tm = _tile(M, 256)
    assert rows_per_mod % tm == 0
    return pl.pallas_call(
        _prenorm_kernel,
        grid=(M // tm,),
        in_specs=[pl.BlockSpec((tm, D), lambda m: (m, 0)),
                  pl.BlockSpec((1, D), lambda m: (0, 0)),
                  _mod_spec(mod, sh_idx, D, tm, rows_per_mod, None),
                  _mod_spec(mod, sc_idx, D, tm, rows_per_mod, None)],
        out_specs=pl.BlockSpec((tm, D), lambda m: (m, 0)),
        out_shape=jax.ShapeDtypeStruct((M, D), BF16),
        compiler_params=_params("arbitrary"),
        name="prenorm",
    )(x, g.reshape(1, D), mod, mod)


class _Riders:
    N_INPUTS = {"cast": 1, "norm": 4}

    def __init__(self, n_steps, step_index):
        self.n_steps = n_steps
        self.step_index = step_index
        self.kinds, self.in_specs, self.args, self.out_specs, self.out_shapes = [], [], [], [], []

    def _slab_spec(self, rows, cols):
        slab = rows // self.n_steps
        assert rows % self.n_steps == 0 and slab % BF16_SUBLANES == 0, (rows, self.n_steps)
        return slab, pl.BlockSpec((slab, cols), lambda *g: (self.step_index(*g), 0))

    def add_cast(self, w):
        _, spec = self._slab_spec(*w.shape)
        self.kinds.append("cast")
        self.in_specs.append(spec)
        self.args.append(w)
        self.out_specs.append(spec)
        self.out_shapes.append(jax.ShapeDtypeStruct(w.shape, BF16))

    def add_norm(self, x, g, mod, sh_idx, sc_idx, rows_per_mod):
        M, D = x.shape
        slab, spec = self._slab_spec(M, D)
        assert rows_per_mod % slab == 0
        per_batch = mod.shape[0] > 1

        def mod_spec(which):
            return pl.BlockSpec((None, 1, D), lambda *g: (
                (self.step_index(*g) * slab) // rows_per_mod if per_batch else 0, 0, which))

        self.kinds.append("norm")
        self.in_specs += [spec, pl.BlockSpec((1, D), lambda *g: (0, 0)), mod_spec(sh_idx), mod_spec(sc_idx)]
        self.args += [x, g.reshape(1, D), mod, mod]
        self.out_specs.append(spec)
        self.out_shapes.append(jax.ShapeDtypeStruct((M, D), BF16))

    @staticmethod
    def run(kinds, in_refs, out_refs):
        i = 0
        for kind, dst in zip(kinds, out_refs):
            srcs = in_refs[i:i + _Riders.N_INPUTS[kind]]
            i += len(srcs)
            if kind == "cast":
                dst[...] = srcs[0][...].astype(dst.dtype)
            else:
                _prenorm_kernel(*srcs, dst)


def _split_refs(refs, n_in, n_out, rider_kinds):
    n_rin = sum(_Riders.N_INPUTS[k] for k in rider_kinds)
    host_out = refs[n_in + n_rin:n_in + n_rin + n_out]
    run = functools.partial(_Riders.run, rider_kinds, refs[n_in:n_in + n_rin], refs[n_in + n_rin + n_out:])
    return refs[:n_in], host_out, run


def _call_with_riders(kernel_fn, riders, n_in, grid, in_specs, out_specs, out_shapes, args, sem, name):
    kinds = ()
    if riders is not None:
        kinds = tuple(riders.kinds)
        in_specs = in_specs + riders.in_specs
        args = args + riders.args
        out_specs = out_specs + riders.out_specs
        out_shapes = out_shapes + riders.out_shapes
    assert len(in_specs) == n_in + sum(_Riders.N_INPUTS[k] for k in kinds)
    out = pl.pallas_call(
        functools.partial(kernel_fn, rider_kinds=kinds),
        grid=grid,
        in_specs=in_specs,
        out_specs=out_specs,
        out_shape=out_shapes,
        compiler_params=_params(*sem),
        name=name,
    )(*args)
    n_host_out = len(out) - len(kinds)
    return list(out[:n_host_out]), list(out[n_host_out:])


def _lane():
    return lax.broadcasted_iota(jnp.int32, (1, HEAD_DIM), 1)


def _head_norm(y, gain, chunk):
    y2 = y * y
    if chunk == HEAD_DIM:
        ms = jnp.mean(y2, axis=-1, keepdims=True)
    else:
        lo = _lane() < chunk
        s0 = jnp.sum(jnp.where(lo, y2, 0.0), axis=-1, keepdims=True)
        s1 = jnp.sum(jnp.where(lo, 0.0, y2), axis=-1, keepdims=True)
        ms = jnp.where(lo, s0, s1) * (1.0 / chunk)
    return y * lax.rsqrt(ms + EPS) * gain


def _rope(y, cos, sin, chunk):
    rot = chunk // 4
    first = (_lane() % (2 * rot)) < rot
    partner = jnp.where(first, pltpu.roll(y, HEAD_DIM - rot, 1), pltpu.roll(y, rot, 1))
    return y * cos + partner * sin


PROJ_ROW_CHUNK = 128


def _row_chunks(rows):
    step = min(rows, PROJ_ROW_CHUNK)
    return [slice(r, r + step) for r in range(0, rows, step)]


def _q_tile(a_ref, w_ref, g_ref, tab, o_ref, chunk):
    scale = chunk ** -0.5 * LOG2E
    for rows in _row_chunks(a_ref.shape[0]):
        acc = jnp.dot(a_ref[rows, :], w_ref[...], preferred_element_type=F32)
        for h in range(o_ref.shape[0] * GROUP):
            y = _head_norm(acc[:, h * HEAD_DIM:(h + 1) * HEAD_DIM], g_ref[...], chunk)
            if tab is not None:
                y = _rope(y, tab[0][rows, :], tab[1][rows, :], chunk)
            o_ref[h // GROUP, h % GROUP, rows, :] = (y * scale).astype(o_ref.dtype)


def _qproj_kernel(*refs, rope, tiles):
    a_ref, w_ref, ga_ref, gb_ref = refs[:4]
    tab_a = tab_b = None
    if rope:
        tab_a, tab_b = refs[4:6], refs[6:8]
    qa_ref, qb_ref = refs[-2:]
    n = pl.program_id(1)

    @pl.when(n < tiles)
    def _():
        _q_tile(a_ref, w_ref, ga_ref, tab_a, qa_ref, HEAD_DIM)

    @pl.when(n >= tiles)
    def _():
        _q_tile(a_ref, w_ref, gb_ref, tab_b, qb_ref, DIFF_DIM)


def _q_proj(h, w_in, cols, n_kv, gains, tables, seq):
    M, D = h.shape
    tm = _tile(seq, 1024) if tables is not None else _tile(M, 1024)
    kv_per_step = 2 if n_kv % 2 == 0 and all(c % (2 * GROUP * HEAD_DIM) == 0 for c in cols) else 1
    tn = kv_per_step * GROUP * HEAD_DIM
    assert all(c % tn == 0 for c in cols)
    tiles = n_kv // kv_per_step
    tile_a, tile_b = cols[0] // tn, cols[1] // tn
    rope = tables is not None
    in_specs = [pl.BlockSpec((tm, D), lambda m, n: (m, 0)),
                pl.BlockSpec((D, tn), lambda m, n: (0, jnp.where(n < tiles, tile_a + n, tile_b + n - tiles))),
                pl.BlockSpec((1, HEAD_DIM), lambda m, n: (0, 0)),
                pl.BlockSpec((1, HEAD_DIM), lambda m, n: (0, 0))]
    args = [h, w_in] + list(gains)
    if rope:
        per_seq = seq // tm
        in_specs += [pl.BlockSpec((tm, HEAD_DIM), lambda m, n: (m % per_seq, 0))] * 4
        args += list(tables[0]) + list(tables[1])

    def out_spec(first):
        return pl.BlockSpec((kv_per_step, GROUP, tm, HEAD_DIM),
                            lambda m, n: (jnp.clip(n - first, 0, tiles - 1), 0, m, 0))

    return pl.pallas_call(
        functools.partial(_qproj_kernel, rope=rope, tiles=tiles),
        grid=(M // tm, 2 * tiles),
        in_specs=in_specs,
        out_specs=[out_spec(0), out_spec(tiles)],
        out_shape=[jax.ShapeDtypeStruct((n_kv, GROUP, M, HEAD_DIM), BF16)] * 2,
        compiler_params=_params("arbitrary", "arbitrary"),
        name="q_proj",
    )(*args)


def _kv_tile(a_ref, w_ref, g_ref, tab, k_ref, v_ref, chunk, n_kv, seq, k_layout, v_layout):
    width = n_kv * HEAD_DIM
    for rows in _row_chunks(a_ref.shape[0]):
        acc = jnp.dot(a_ref[rows, :], w_ref[...], preferred_element_type=F32)
        for h in range(n_kv):
            y = _head_norm(acc[:, h * HEAD_DIM:(h + 1) * HEAD_DIM], g_ref[...], chunk)
            if tab is not None:
                y = _rope(y, tab[0][rows, :], tab[1][rows, :], chunk)
            _store_head(k_ref, k_layout, y, h, n_kv, rows, seq)
            _store_head(v_ref, v_layout, acc[:, width + h * HEAD_DIM:width + (h + 1) * HEAD_DIM], h, n_kv, rows, seq)


def _kvproj_kernel(*refs, rope, n_kv, seq, kv_layout):
    a_ref, w_ref, ga_ref, gb_ref = refs[:4]
    tab_a = tab_b = None
    if rope:
        tab_a, tab_b = refs[4:6], refs[6:8]
    ka_ref, va_ref, ktb_ref, vb_ref = refs[-4:]
    n = pl.program_id(1)

    @pl.when(n == 0)
    def _():
        _kv_tile(a_ref, w_ref, ga_ref, tab_a, ka_ref, va_ref, HEAD_DIM, n_kv, seq, kv_layout, kv_layout)

    @pl.when(n == 1)
    def _():
        _kv_tile(a_ref, w_ref, gb_ref, tab_b, ktb_ref, vb_ref, DIFF_DIM, n_kv, seq, "transposed", kv_layout)


def _store_head(ref, layout, y, h, n_kv, rows, seq):
    y = y.astype(ref.dtype) if layout != "transposed" else y
    if layout == "rows":
        ref[rows, h * HEAD_DIM:(h + 1) * HEAD_DIM] = y
    elif layout == "heads":
        ref[pl.ds(rows.start * n_kv + h, rows.stop - rows.start, stride=n_kv), :] = y
    else:
        s, t0 = divmod(rows.start, seq)
        ref[s, h * HEAD_DIM:(h + 1) * HEAD_DIM, t0:t0 + rows.stop - rows.start] = y.T.astype(ref.dtype)


def _load_head(ref, layout, h, n_kv, t0, n):
    if layout == "rows":
        return ref[pl.ds(t0, n), h * HEAD_DIM:(h + 1) * HEAD_DIM]
    return ref[pl.ds(t0 * n_kv + h, n, stride=n_kv), :]


def _head_spec(layout, n_kv, rows, index):
    shape = (rows, n_kv * HEAD_DIM) if layout == "rows" else (rows * n_kv, HEAD_DIM)
    return pl.BlockSpec(shape, lambda *g: (index(*g), 0))


def _kv_proj(h, w_in, cols, n_kv, gains, tables, seq, out_dtype, kv_layout):
    M, D = h.shape
    tm = _tile(seq, 1024) if tables is not None else _tile(M, 1024)
    width = n_kv * HEAD_DIM
    tn = 2 * width
    assert all(c % tn == 0 for c in cols) and tm % seq == 0 and seq % PROJ_ROW_CHUNK == 0
    tile_a, tile_b = cols[0] // tn, cols[1] // tn
    rope = tables is not None

    def out(layout):
        if layout == "transposed":
            return (pl.BlockSpec((tm // seq, width, seq), lambda m, n: (m, 0, 0)),
                    jax.ShapeDtypeStruct((M // seq, width, seq), out_dtype))
        shape = (M, width) if layout == "rows" else (M * n_kv, HEAD_DIM)
        return _head_spec(layout, n_kv, tm, lambda m, n: m), jax.ShapeDtypeStruct(shape, out_dtype)

    outs = [out(kv_layout), out(kv_layout), out("transposed"), out(kv_layout)]
    in_specs = [pl.BlockSpec((tm, D), lambda m, n: (m, 0)),
                pl.BlockSpec((D, tn), lambda m, n: (0, jnp.where(n == 0, tile_a, tile_b))),
                pl.BlockSpec((1, HEAD_DIM), lambda m, n: (0, 0)),
                pl.BlockSpec((1, HEAD_DIM), lambda m, n: (0, 0))]
    args = [h, w_in] + list(gains)
    if rope:
        per_seq = seq // tm
        in_specs += [pl.BlockSpec((tm, HEAD_DIM), lambda m, n: (m % per_seq, 0))] * 4
        args += list(tables[0]) + list(tables[1])
    return pl.pallas_call(
        functools.partial(_kvproj_kernel, rope=rope, n_kv=n_kv, seq=seq, kv_layout=kv_layout),
        grid=(M // tm, 2),
        in_specs=in_specs,
        out_specs=[o[0] for o in outs],
        out_shape=[o[1] for o in outs],
        compiler_params=_params("arbitrary", "arbitrary"),
        name="kv_proj",
    )(*args)


def _with_ones(v):
    return jnp.concatenate([v, jnp.ones_like(v)], axis=1)


def _weights(s, m):
    return jnp.exp2((s - m).astype(BF16))


def _softmax_sink_pv(scores, values_ones, sink):
    m = sink
    for s in scores:
        m = jnp.maximum(m, jnp.max(s, axis=-1, keepdims=True))
    ox = None
    for s, vx in zip(scores, values_ones):
        pv = jnp.dot(_weights(s, m), vx, preferred_element_type=F32)
        ox = pv if ox is None else ox + pv
    denom = ox[:, HEAD_DIM:] + jnp.exp2(sink - m)
    return ox[:, :HEAD_DIM] / denom


def _qk(q, k):
    return lax.dot_general(q, k, (((1,), (1,)), ((), ())), preferred_element_type=F32)


def _attn_a_ctx_kernel(*refs, n_kv, seq, rider_kinds):
    (sink_ref, q_ref, k_ref, v_ref), (o_ref,), run_riders = _split_refs(refs, 4, 1, rider_kinds)
    run_riders()
    for h in range(n_kv):
        k = _load_head(k_ref, "heads", h, n_kv, 0, seq).astype(BF16)
        vx = _with_ones(_load_head(v_ref, "heads", h, n_kv, 0, seq).astype(BF16))
        for g in range(GROUP):
            head = h * GROUP + g
            sink = jnp.full((1, 1), sink_ref[head] * LOG2E, F32)
            o = _softmax_sink_pv([_qk(q_ref[h, g], k)], [vx], sink)
            o_ref[:, head * HEAD_DIM:(head + 1) * HEAD_DIM] = o.astype(o_ref.dtype)


def _attn_a_ctx(q, k, v, sink, batch, seq, attach=None):
    n_kv, _, M, _ = q.shape
    width = n_kv * HEAD_DIM
    in_specs = [pl.BlockSpec(memory_space=pltpu.SMEM),
                pl.BlockSpec((n_kv, GROUP, seq, HEAD_DIM), lambda b: (0, 0, b, 0)),
                _head_spec("heads", n_kv, seq, lambda b: b),
                _head_spec("heads", n_kv, seq, lambda b: b)]
    riders = _Riders(batch, lambda b: b)
    if attach is not None:
        attach(riders)
    (o,), rider_out = _call_with_riders(
        functools.partial(_attn_a_ctx_kernel, n_kv=n_kv, seq=seq), riders, len(in_specs), (batch,), in_specs,
        [pl.BlockSpec((seq, GROUP * width), lambda b: (b, 0))],
        [jax.ShapeDtypeStruct((M, GROUP * width), BF16)],
        [sink, q, k, v], ("arbitrary",), "attn_a_ctx")
    return o, rider_out


ATTN_A_BLOCKS_PER_STEP = 4


def _attn_a_lat_kernel(*refs, n_blocks, n_kv, past, per_step, rider_kinds):
    (sink_ref, q_ref, k_ref, v_ref, ck_ref, cv_ref), (o_ref,), run_riders = _split_refs(refs, 6, 1, rider_kinds)
    run_riders()
    rows = GROUP * BLOCK
    span = 3 * BLOCK
    row = lax.broadcasted_iota(jnp.int32, (rows, 1), 0)
    lane = lax.broadcasted_iota(jnp.int32, (1, span), 1)

    for u in range(per_step):
        j = pl.program_id(1) * per_step + u
        start = pl.multiple_of(jnp.clip(j - 1, 0, n_blocks - 3) * BLOCK, BLOCK)
        in_window = jnp.abs(start + lane - (j * BLOCK + row % BLOCK)) <= WINDOW
        q_rows = slice(u * BLOCK, (u + 1) * BLOCK)
        for h in range(n_kv):
            kw = _load_head(k_ref, "rows", h, n_kv, start, span)
            vw = _with_ones(_load_head(v_ref, "rows", h, n_kv, start, span))
            kc = _load_head(ck_ref, "heads", h, n_kv, 0, past).astype(BF16)
            vc = _with_ones(_load_head(cv_ref, "heads", h, n_kv, 0, past).astype(BF16))
            q = q_ref[h, :, q_rows, :].reshape(rows, HEAD_DIM)
            s_w = jnp.where(in_window, _qk(q, kw), NEG)
            s_c = _qk(q, kc)
            sink = jnp.zeros((rows, 1), F32)
            for g in range(GROUP):
                sink = jnp.where(row // BLOCK == g, sink_ref[h * GROUP + g] * LOG2E, sink)
            o = _softmax_sink_pv([s_w, s_c], [vw, vc], sink)
            for g in range(GROUP):
                head = h * GROUP + g
                o_ref[q_rows, head * HEAD_DIM:(head + 1) * HEAD_DIM] = (
                    o[g * BLOCK:(g + 1) * BLOCK].astype(o_ref.dtype))


def _attn_a_lat(q, k, v, ck, cv, sink, batch, seq, attach=None):
    n_kv, _, M, _ = q.shape
    n_blocks = seq // BLOCK
    per_step = ATTN_A_BLOCKS_PER_STEP if n_blocks % ATTN_A_BLOCKS_PER_STEP == 0 else 1
    n_steps = n_blocks // per_step
    assert n_blocks >= 3
    past = ck.shape[1] // n_kv
    width = n_kv * HEAD_DIM
    tq = per_step * BLOCK
    in_specs = [pl.BlockSpec(memory_space=pltpu.SMEM),
                pl.BlockSpec((n_kv, GROUP, tq, HEAD_DIM), lambda b, j: (0, 0, b * n_steps + j, 0)),
                _head_spec("rows", n_kv, seq, lambda b, j: b),
                _head_spec("rows", n_kv, seq, lambda b, j: b),
                pl.BlockSpec((None, past * n_kv, HEAD_DIM), lambda b, j: (b, 0, 0)),
                pl.BlockSpec((None, past * n_kv, HEAD_DIM), lambda b, j: (b, 0, 0))]
    riders = _Riders(batch * n_steps, lambda b, j: b * n_steps + j)
    if attach is not None:
        attach(riders)
    (o,), rider_out = _call_with_riders(
        functools.partial(_attn_a_lat_kernel, n_blocks=n_blocks, n_kv=n_kv, past=past, per_step=per_step),
        riders, len(in_specs), (batch, n_steps), in_specs,
        [pl.BlockSpec((tq, GROUP * width), lambda b, j: (b * n_steps + j, 0))],
        [jax.ShapeDtypeStruct((M, GROUP * width), BF16)],
        [sink, q, k, v, ck, cv], ("arbitrary",) * 2, "attn_a_lat")
    return o, rider_out


ATTN_B_Q_ROWS = 256
ATTN_B_TILES_PER_STEP = 2


def _attn_b_kernel(*refs, has_ctx, lam_init, n_kv, seq, past, v_layout, rider_kinds):
    ins, (o_ref,), run_riders = _split_refs(refs, 7 if has_ctx else 5, 1, rider_kinds)
    run_riders()
    if has_ctx:
        lam_ref, subg_ref, q_ref, kt_ref, v_ref, ckt_ref, cv_ref = ins
    else:
        lam_ref, subg_ref, q_ref, kt_ref, v_ref = ins
    h = pl.program_id(1)

    lv = lam_ref[...]
    lam = (jnp.exp(jnp.sum(lv[0:1] * lv[1:2], axis=-1, keepdims=True))
           - jnp.exp(jnp.sum(lv[2:3] * lv[3:4], axis=-1, keepdims=True)) + lam_init)

    kt = kt_ref[...].astype(F32)
    if v_layout == "rows":
        v = v_ref[...]
    else:
        v = _load_head(v_ref, "heads", h, n_kv, 0, seq).astype(BF16)
    if has_ctx:
        kt = jnp.concatenate([kt, ckt_ref[...]], axis=1)
        v = jnp.concatenate([v, _load_head(cv_ref, "heads", h, n_kv, 0, past).astype(BF16)], axis=0)
    vx = _with_ones(v)
    lo = lax.broadcasted_iota(jnp.int32, (HEAD_DIM, 1), 0) < DIFF_DIM
    kt0 = jnp.where(lo, kt, 0.0).astype(BF16)
    kt1 = jnp.where(lo, 0.0, kt).astype(BF16)

    tq = q_ref.shape[1]
    for r in range(0, tq, min(tq, ATTN_B_Q_ROWS)):
        q_rows = slice(r, r + min(tq, ATTN_B_Q_ROWS))
        for g in range(GROUP):
            q = q_ref[g, q_rows, :]
            maps = []
            for ktc in (kt0, kt1):
                s = jnp.dot(q, ktc, preferred_element_type=F32)
                ox = jnp.dot(_weights(s, jnp.max(s, axis=-1, keepdims=True)), vx, preferred_element_type=F32)
                maps.append(ox[:, :HEAD_DIM] / ox[:, HEAD_DIM:])
            o = maps[0] - lam * maps[1]
            ms = jnp.mean(o * o, axis=-1, keepdims=True)
            o = o * lax.rsqrt(ms + EPS) * subg_ref[...] * (1.0 - lam_init)
            o_ref[q_rows, g * HEAD_DIM:(g + 1) * HEAD_DIM] = o.astype(o_ref.dtype)


def _attn_b(q, kt, v, v_layout, ctx, lam_vecs, sub_g, lam_init, batch, seq, attach=None):
    n_kv, _, M, _ = q.shape
    tq = _tile(seq, ATTN_B_Q_ROWS * ATTN_B_TILES_PER_STEP)
    n_q = seq // tq
    has_ctx = ctx is not None
    n_steps = batch * n_kv * n_q
    if v_layout == "rows":
        v_spec = pl.BlockSpec((seq, HEAD_DIM), lambda b, h, j: (b, h))
    else:
        v_spec = _head_spec("heads", n_kv, seq, lambda b, h, j: b)
    in_specs = [pl.BlockSpec((4, DIFF_DIM), lambda b, h, j: (0, 0)),
                pl.BlockSpec((1, HEAD_DIM), lambda b, h, j: (0, 0)),
                pl.BlockSpec((None, GROUP, tq, HEAD_DIM), lambda b, h, j: (h, 0, b * n_q + j, 0)),
                pl.BlockSpec((None, HEAD_DIM, seq), lambda b, h, j: (b, h, 0)),
                v_spec]
    args = [lam_vecs, sub_g, q, kt, v]
    past = 0
    if has_ctx:
        past = ctx[0].shape[2]
        in_specs += [pl.BlockSpec((None, HEAD_DIM, past), lambda b, h, j: (b, h, 0)),
                     pl.BlockSpec((None, past * n_kv, HEAD_DIM), lambda b, h, j: (b, 0, 0))]
        args += list(ctx)
    out_specs = [pl.BlockSpec((tq, GROUP * HEAD_DIM), lambda b, h, j: (b * n_q + j, h))]
    out_shape = [jax.ShapeDtypeStruct((M, n_kv * GROUP * HEAD_DIM), BF16)]
    riders = _Riders(n_steps, lambda b, h, j: (b * n_kv + h) * n_q + j)
    if attach is not None:
        attach(riders)
    (o,), rider_out = _call_with_riders(
        functools.partial(_attn_b_kernel, has_ctx=has_ctx, lam_init=lam_init, n_kv=n_kv, seq=seq, past=past,
                          v_layout=v_layout),
        riders, len(in_specs), (batch, n_kv, n_q), in_specs, out_specs, out_shape, args,
        ("arbitrary",) * 3, "attn_b")
    return o, rider_out


def _oproj_kernel(oa_ref, ob_ref, wa_ref, wb_ref, x_ref, gt_ref, y_ref):
    acc = jnp.dot(oa_ref[...], wa_ref[...], preferred_element_type=F32)
    acc = acc + jnp.dot(ob_ref[...], wb_ref[...], preferred_element_type=F32)
    y_ref[...] = x_ref[...] + gt_ref[...] * acc


def _out_proj(o_a, o_b, w_out, x, mod, gt_idx, rows_per_mod):
    M, D = x.shape
    wa = o_a.shape[1]
    wb = o_b.shape[1]
    assert wa == wb and wa + wb == w_out.shape[0]
    tm = _tile(rows_per_mod, 1024) if mod.shape[0] > 1 else _tile(M, 1024)
    tn = _tile(D, 1024)
    return pl.pallas_call(
        _oproj_kernel,
        grid=(M // tm, D // tn),
        in_specs=[pl.BlockSpec((tm, wa), lambda m, n: (m, 0)),
                  pl.BlockSpec((tm, wb), lambda m, n: (m, 0)),
                  pl.BlockSpec((wa, tn), lambda m, n: (0, n)),
                  pl.BlockSpec((wb, tn), lambda m, n: (1, n)),
                  pl.BlockSpec((tm, tn), lambda m, n: (m, n)),
                  _mod_spec(mod, gt_idx, tn, tm, rows_per_mod, 0)],
        out_specs=pl.BlockSpec((tm, tn), lambda m, n: (m, n)),
        out_shape=jax.ShapeDtypeStruct((M, D), F32),
        compiler_params=_params("arbitrary", "arbitrary"),
        name="out_proj",
    )(o_a, o_b, w_out, w_out, x, mod)


def _fc1_kernel(*refs, rider_kinds):
    (a_ref, w_ref), (o_ref,), run_riders = _split_refs(refs, 2, 1, rider_kinds)
    run_riders()
    acc = jnp.dot(a_ref[...], w_ref[...], preferred_element_type=F32)
    r = jnp.maximum(acc, 0.0)
    o_ref[...] = (r * r).astype(o_ref.dtype)


def _fc1(h, w, attach=None):
    M, D = h.shape
    N = w.shape[1]
    tm = _tile(M, 1024)
    tn = _tile(N, 1024)
    n_n = N // tn
    riders = _Riders((M // tm) * n_n, lambda m, n: m * n_n + n)
    if attach is not None:
        attach(riders)
    (hid,), rider_out = _call_with_riders(
        _fc1_kernel, riders, 2, (M // tm, n_n),
        [pl.BlockSpec((tm, D), lambda m, n: (m, 0)), pl.BlockSpec((D, tn), lambda m, n: (0, n))],
        [pl.BlockSpec((tm, tn), lambda m, n: (m, n))],
        [jax.ShapeDtypeStruct((M, N), BF16)],
        [h, w], ("arbitrary",) * 2, "fc1")
    return hid, rider_out


def _fc2_kernel(a_ref, w_ref, x_ref, gt_ref, y_ref):
    k = pl.program_id(2)
    last = pl.num_programs(2) - 1

    def part():
        return jnp.dot(a_ref[...], w_ref[...], preferred_element_type=F32)

    @pl.when(k == 0)
    def _():
        y_ref[...] = part()

    @pl.when(jnp.logical_and(k > 0, k < last))
    def _():
        y_ref[...] += part()

    @pl.when(k == last)
    def _():
        y_ref[...] = x_ref[...] + gt_ref[...] * (y_ref[...] + part())


def _fc2(hid, w, x, mod, gt_idx, rows_per_mod):
    M, D = x.shape
    K = hid.shape[1]
    tm = _tile(rows_per_mod, 1024) if mod.shape[0] > 1 else _tile(M, 1024)
    tn = _tile(D, 1024)
    tk = _tile(K, 4096)
    assert K // tk >= 2
    return pl.pallas_call(
        _fc2_kernel,
        grid=(M // tm, D // tn, K // tk),
        in_specs=[pl.BlockSpec((tm, tk), lambda m, n, k: (m, k)),
                  pl.BlockSpec((tk, tn), lambda m, n, k: (k, n)),
                  pl.BlockSpec((tm, tn), lambda m, n, k: (m, n)),
                  _mod_spec(mod, gt_idx, tn, tm, rows_per_mod, 0)],
        out_specs=pl.BlockSpec((tm, tn), lambda m, n, k: (m, n)),
        out_shape=jax.ShapeDtypeStruct((M, D), F32),
        compiler_params=_params("arbitrary", "arbitrary", "arbitrary"),
        name="fc2",
    )(hid, w, x, mod)


def _rope_tables(seq, chunk):
    n = chunk // 4
    pos = jnp.arange(seq)
    row = (pos // GRID_W).astype(F32)
    col = (pos % GRID_W).astype(F32)
    freq = ROPE_BASE ** (-jnp.arange(n, dtype=F32) / n)
    ra = row[:, None] * freq
    ca = col[:, None] * freq
    cos = jnp.concatenate([jnp.cos(ra)] * 2 + [jnp.cos(ca)] * 2, axis=-1)
    sin = jnp.concatenate([-jnp.sin(ra), jnp.sin(ra), -jnp.sin(ca), jnp.sin(ca)], axis=-1)
    reps = HEAD_DIM // chunk
    return jnp.tile(cos, (1, reps)), jnp.tile(sin, (1, reps))


def _mixer(h, x, mod, w, l, batch, seq, n_kv_a, n_kv_b, ctx_kv, attach_a=None, attach_b=None):
    latent = ctx_kv is not None
    wa = n_kv_a * GROUP * HEAD_DIM
    ka = n_kv_a * HEAD_DIM
    wb = n_kv_b * GROUP * HEAD_DIM
    tab_a = _rope_tables(seq, HEAD_DIM) if latent else None
    tab_b = _rope_tables(seq, DIFF_DIM) if latent else None
    kv_dtype = BF16 if latent else F32
    kv_layout = "rows" if latent else "heads"

    gq_a = w["qnorm_a_g"].reshape(1, HEAD_DIM)
    gk_a = w["knorm_a_g"].reshape(1, HEAD_DIM)
    gq_b = jnp.tile(w["qnorm_b_g"], 2).reshape(1, HEAD_DIM)
    gk_b = jnp.tile(w["knorm_b_g"], 2).reshape(1, HEAD_DIM)
    assert n_kv_a == n_kv_b
    tables = (tab_a, tab_b) if latent else None
    q_a, q_b = _q_proj(h, w["w_in"], (0, wa + 2 * ka), n_kv_a, (gq_a, gq_b), tables, seq)
    k_a, v_a, kt_b, v_b = _kv_proj(h, w["w_in"], (wa, wa + 2 * ka + wb), n_kv_a, (gk_a, gk_b), tables, seq,
                                   kv_dtype, kv_layout)

    lam_vecs = jnp.stack([w["lam_q1"], w["lam_k1"], w["lam_q2"], w["lam_k2"]]).astype(F32)
    lam_init = 0.8 - 0.6 * math.exp(-0.3 * l)
    sub_g = w["subln_g"].reshape(1, HEAD_DIM)
    w_out = w["w_out"]

    def attach_a_all(r):
        if attach_a is not None:
            attach_a(r)
        if w_out.dtype != BF16:
            r.add_cast(w_out)

    if latent:
        ck_a, cv_a, ckt_b, cv_b = ctx_kv
        o_a, ride_a = _attn_a_lat(q_a, k_a, v_a, ck_a, cv_a, w["sink_a"], batch, seq, attach_a_all)
        ctx_b = (ckt_b, cv_b)
    else:
        o_a, ride_a = _attn_a_ctx(q_a, k_a, v_a, w["sink_a"], batch, seq, attach_a_all)
        ctx_b = None
    if w_out.dtype != BF16:
        w_out = ride_a[-1]
    o_b, ride_b = _attn_b(q_b, kt_b, v_b, kv_layout, ctx_b, lam_vecs, sub_g, lam_init, batch, seq, attach_b)
    x = _out_proj(o_a, o_b, w_out, x, mod, 2, seq)
    return x, (k_a, v_a, kt_b, v_b), ride_a, ride_b


def _mlp(h2, x, mod, w_fc1, w_fc2, seq, attach=None):
    hid, ride = _fc1(h2, w_fc1, attach)
    return _fc2(hid, w_fc2, x, mod, 5, seq), ride


def kernel(x_prompt, x_sample, cache_a_k, cache_a_v, cache_b_k, cache_b_v, c, c_ctx, w_mod, b_mod, norm1_g, w_in, qnorm_a_g, knorm_a_g, qnorm_b_g, knorm_b_g, sink_a, lam_q1, lam_k1, lam_q2, lam_k2, subln_g, w_out, norm2_g, w_fc1, w_fc2):
    batch, seq, D = x_prompt.shape
    dbatch, dseq, _ = x_sample.shape
    depth = w_mod.shape[0]
    past = cache_a_k.shape[2]
    n_kv_a = cache_a_k.shape[3]
    n_kv_b = cache_b_k.shape[3]
    assert dseq % GRID_W == 0 and dseq % BLOCK == 0

    per_layer = dict(norm1_g=norm1_g, qnorm_a_g=qnorm_a_g, knorm_a_g=knorm_a_g, qnorm_b_g=qnorm_b_g,
                     knorm_b_g=knorm_b_g, sink_a=sink_a, lam_q1=lam_q1, lam_k1=lam_k1, lam_q2=lam_q2,
                     lam_k2=lam_k2, subln_g=subln_g, norm2_g=norm2_g)

    n_rows = -(-(dbatch + 1) // 8) * 8
    cvecs = jnp.zeros((n_rows, D), F32).at[:dbatch].set(c).at[dbatch].set(c_ctx)

    xp = x_prompt.reshape(batch * seq, D)
    xs = x_sample.reshape(dbatch * dseq, D)
    new_kv = []
    for l in range(depth):
        w = {k: v[l] for k, v in per_layer.items()}
        w["w_in"] = w_in[l].astype(BF16)
        w["w_out"] = w_out[l]
        mod = _modulation(cvecs, w_mod[l], b_mod[l])
        mod_ctx = mod[dbatch:dbatch + 1].reshape(1, 1, 6 * D)
        mod_lat = mod[:dbatch].reshape(dbatch, 1, 6 * D)

        xs_in = xs
        hp = _prenorm(xp, w["norm1_g"], mod_ctx, 0, 1, seq)
        xp, kv, (w_out_bf16,), (hs,) = _mixer(
            hp, xp, mod_ctx, w, l, batch, seq, n_kv_a, n_kv_b, None,
            attach_b=lambda r: r.add_norm(xs_in, w["norm1_g"], mod_lat, 0, 1, dseq))
        w["w_out"] = w_out_bf16
        new_kv.append(kv)
        ctx_kv = (cache_a_k[:, l].reshape(dbatch, past * n_kv_a, HEAD_DIM),
                  cache_a_v[:, l].reshape(dbatch, past * n_kv_a, HEAD_DIM),
                  jnp.transpose(cache_b_k[:, l], (0, 2, 3, 4, 1)).reshape(dbatch, n_kv_b * HEAD_DIM, past),
                  cache_b_v[:, l].reshape(dbatch, past * n_kv_b, HEAD_DIM))
        xp_mid = xp

        def attach_casts(r):
            r.add_cast(w_fc1[l])
            r.add_cast(w_fc2[l])

        xs, _, (hp2,), (fc1_bf16, fc2_bf16) = _mixer(
            hs, xs, mod_lat, w, l, dbatch, dseq, n_kv_a, n_kv_b, ctx_kv,
            attach_a=lambda r: r.add_norm(xp_mid, w["norm2_g"], mod_ctx, 3, 4, seq),
            attach_b=attach_casts)
        xs_mid = xs
        xp, (hs2,) = _mlp(hp2, xp, mod_ctx, fc1_bf16, fc2_bf16, seq,
                          attach=lambda r: r.add_norm(xs_mid, w["norm2_g"], mod_lat, 3, 4, dseq))
        xs, _ = _mlp(hs2, xs, mod_lat, fc1_bf16, fc2_bf16, dseq)

    def stack(i, to_cache_shape):
        return jnp.stack([to_cache_shape(kv[i]) for kv in new_kv], axis=1)

    def heads_a(t):
        return t.reshape(batch, seq, n_kv_a, HEAD_DIM)

    def heads_b(t):
        return t.reshape(batch, seq, n_kv_b, HEAD_DIM)

    def transposed_b(t):
        return jnp.transpose(t.reshape(batch, n_kv_b, 2, DIFF_DIM, seq), (0, 4, 1, 2, 3))

    return (xp.reshape(batch, seq, D), xs.reshape(dbatch, dseq, D),
            stack(0, heads_a), stack(1, heads_a), stack(2, transposed_b), stack(3, heads_b))
```
